```python
import jax, jax.numpy as jnp
from jax import lax
import numpy as np

D_MODEL = 1024
BATCH = 32
SEQ = 2048
DEPTH = 1

N_META = 16
D_CONV = D_MODEL
CONV_WIDTH = 3
GLA_HEADS = 4
DK = D_MODEL // 2
DV = D_MODEL
HEAD_K = DK // GLA_HEADS
HEAD_V = DV // GLA_HEADS
GATE_RANK = 16
GATE_NORMALIZER = 16.0
CHUNK = 64
EPS = 1e-6
IN_SPLITS = (D_CONV, D_CONV, D_CONV, D_CONV, DK, DK, DV, DV, GATE_RANK, GATE_RANK, D_MODEL, D_MODEL)
N_IN = sum(IN_SPLITS)

kernel_name = "hybrid_gated_shortconv_bigla_block"


def rms_norm(x, g):
    xf = x.astype(jnp.float32)
    y = xf * lax.rsqrt(jnp.mean(xf * xf, axis=-1, keepdims=True) + EPS)
    return (y * g.astype(jnp.float32)).astype(x.dtype)


def short_conv_centred(u, w):
    half = CONV_WIDTH // 2
    L = u.shape[1]
    up = jnp.pad(u, ((0, 0), (half, half), (0, 0)))
    return sum(up[:, i:i + L] * w[i] for i in range(CONV_WIDTH))


def to_chunks(t, pad_front, pad_back, n_heads, head_dim):
    t = jnp.pad(t, ((0, 0), (pad_front, pad_back), (0, 0)))
    bn, lp, _ = t.shape
    t = t.reshape(bn, lp // CHUNK, CHUNK, n_heads, head_dim)
    return t.transpose(0, 3, 1, 2, 4)


def gla_chunked(q, k, v, g, strict):
    bn, nh, _, c, dk = q.shape
    dv = v.shape[-1]
    b = jnp.cumsum(g.astype(jnp.float32), axis=3)
    q_in = q * jnp.exp(b)
    k_in = k * jnp.exp(-b)
    scores = jnp.einsum('bhncd,bhnjd->bhncj', q_in, k_in)
    mask = jnp.tril(jnp.ones((c, c), dtype=bool), k=-1 if strict else 0)
    scores = jnp.where(mask, scores, 0.0)
    o_intra = jnp.einsum('bhncj,bhnje->bhnce', scores, v)
    b_last = b[..., -1:, :]
    k_dec = k * jnp.exp(b_last - b)
    decay = jnp.exp(b_last[..., 0, :])

    def step(state, xs):
        q_n, k_n, v_n, d_n = xs
        o_n = jnp.einsum('bhcd,bhde->bhce', q_n, state)
        state = state * d_n[..., None] + jnp.einsum('bhcd,bhce->bhde', k_n, v_n)
        return state, o_n

    xs = tuple(jnp.moveaxis(t, 2, 0) for t in (q_in, k_dec, v, decay))
    s0 = jnp.zeros((bn, nh, dk, dv), jnp.float32)
    _, o_inter = lax.scan(step, s0, xs)
    return o_intra + jnp.moveaxis(o_inter, 0, 2)


def hybrid_layer(h, g_pre, w_in, conv_w, w_gate_f, b_gate_f, w_gate_b, b_gate_b,
                 gla_g, w_out_c, w_out_g, w_out, g_post):
    bn, L, _ = h.shape
    pad_front = (-N_META) % CHUNK
    pad_back = (-(L - N_META)) % CHUNK
    u = rms_norm(h, g_pre)
    proj = jnp.einsum('bld,dn->bln', u, w_in)
    split_idx = np.cumsum(IN_SPLITS)[:-1].tolist()
    (c_b, c_c, c_x, c_z, q, k, v, r, lr_f, lr_b, m_a, m_b) = jnp.split(proj, split_idx, axis=-1)

    y_conv = c_b * short_conv_centred(c_c * c_x, conv_w) * jax.nn.silu(c_z)
    p_conv = jnp.einsum('blc,cd->bld', y_conv, w_out_c)

    g_f = jax.nn.log_sigmoid((lr_f @ w_gate_f + b_gate_f).astype(jnp.float32)) / GATE_NORMALIZER
    g_b = jax.nn.log_sigmoid((lr_b @ w_gate_b + b_gate_b).astype(jnp.float32)) / GATE_NORMALIZER
    qc = to_chunks(q * (HEAD_K ** -0.5), pad_front, pad_back, GLA_HEADS, HEAD_K)
    kc = to_chunks(k, pad_front, pad_back, GLA_HEADS, HEAD_K)
    vc = to_chunks(v, pad_front, pad_back, GLA_HEADS, HEAD_V)
    gfc = to_chunks(g_f, pad_front, pad_back, GLA_HEADS, HEAD_K)
    gbc = to_chunks(g_b, pad_front, pad_back, GLA_HEADS, HEAD_K)
    rev = lambda t: jnp.flip(t, axis=(2, 3))
    o_f = gla_chunked(qc, kc, vc, gfc, strict=False)
    o_b = rev(gla_chunked(rev(qc), rev(kc), rev(vc), rev(gbc), strict=True))
    o = (o_f + o_b).transpose(0, 2, 3, 1, 4).reshape(bn, -1, GLA_HEADS, HEAD_V)
    o = o[:, pad_front:pad_front + L]
    o = rms_norm(o, gla_g).reshape(bn, L, DV).astype(h.dtype)
    y_gla = o * jax.nn.silu(r)
    p_gla = jnp.einsum('blc,cd->bld', y_gla, w_out_g)

    merged = jax.nn.sigmoid(m_a) * p_conv + jax.nn.sigmoid(m_b) * p_gla
    out = jnp.einsum('bld,de->ble', merged, w_out)
    return h + rms_norm(out, g_post)


def setup_inputs(seed: int = 0) -> dict:
    key = jax.random.key(seed)
    ks = jax.random.split(key, 16)
    nrm = lambda k, shape, scale: jax.random.normal(k, shape, jnp.float32) * scale
    return {
        "x": nrm(ks[0], (BATCH, SEQ, D_MODEL), 1.0),
        "meta_tokens": nrm(ks[1], (N_META, D_MODEL), 1.0),
        "norm_pre": 1.0 + nrm(ks[2], (DEPTH, D_MODEL), 0.05),
        "w_in": nrm(ks[3], (DEPTH, D_MODEL, N_IN), D_MODEL ** -0.5),
        "conv_w": nrm(ks[4], (DEPTH, CONV_WIDTH, D_CONV), CONV_WIDTH ** -0.5),
        "w_gate_fwd": nrm(ks[5], (DEPTH, GATE_RANK, DK), GATE_RANK ** -0.5),
        "b_gate_fwd": nrm(ks[6], (DEPTH, DK), 0.1),
        "w_gate_bwd": nrm(ks[7], (DEPTH, GATE_RANK, DK), GATE_RANK ** -0.5),
        "b_gate_bwd": nrm(ks[8], (DEPTH, DK), 0.1),
        "gla_norm": 1.0 + nrm(ks[9], (DEPTH, HEAD_V), 0.05),
        "w_out_conv": nrm(ks[10], (DEPTH, D_CONV, D_MODEL), D_CONV ** -0.5),
        "w_out_gla": nrm(ks[11], (DEPTH, DV, D_MODEL), DV ** -0.5),
        "w_merge_out": nrm(ks[12], (DEPTH, D_MODEL, D_MODEL), D_MODEL ** -0.5),
        "norm_post": 1.0 + nrm(ks[13], (DEPTH, D_MODEL), 0.05),
    }


def reference(x, meta_tokens, norm_pre, w_in, conv_w, w_gate_fwd, b_gate_fwd, w_gate_bwd,
              b_gate_bwd, gla_norm, w_out_conv, w_out_gla, w_merge_out, norm_post):
    bn = x.shape[0]
    meta = jnp.broadcast_to(meta_tokens[None].astype(x.dtype), (bn, N_META, D_MODEL))
    h = jnp.concatenate([meta, x], axis=1)
    for l in range(DEPTH):
        h = hybrid_layer(h, norm_pre[l], w_in[l], conv_w[l], w_gate_fwd[l], b_gate_fwd[l],
                         w_gate_bwd[l], b_gate_bwd[l], gla_norm[l], w_out_conv[l],
                         w_out_gla[l], w_merge_out[l], norm_post[l])
    return h[:, N_META:]
```

```python
import functools

import jax
import jax.numpy as jnp
from jax import lax
from jax.experimental import pallas as pl
from jax.experimental.pallas import tpu as pltpu

D_MODEL = 1024
N_META = 16
HEADS = 4
DK = 512
DV = 1024
HEAD_K = DK // HEADS
HEAD_V = DV // HEADS
GATE_RANK = 16
GATE_NORMALIZER = 16.0
CHUNK = 64
EPS = 1e-6

COL_CB, COL_CC, COL_CX, COL_CZ = 0, 1024, 2048, 3072
COL_Q, COL_K, COL_V, COL_R = 4096, 4608, 5120, 6144
COL_MA, COL_MB, COL_LR = 7168, 8192, 9216
LR_PAD = 128
N_COLS = COL_LR + LR_PAD

HALO = 16
TILE1 = 512
TILE2 = 256
CW = 256
VMEM_LIMIT = 56 * 1024 * 1024

F32 = jnp.float32
BF16 = jnp.bfloat16


def _dot(a, b):
    return jnp.dot(a, b, preferred_element_type=F32)


def _dot_nt(a, b):
    return lax.dot_general(a, b, (((1,), (1,)), ((), ())), preferred_element_type=F32)


def _dot_tn(a, b):
    return lax.dot_general(a, b, (((0,), (0,)), ((), ())), preferred_element_type=F32)


def _rms(x, g):
    ms = jnp.mean(x * x, axis=-1, keepdims=True)
    return x * lax.rsqrt(ms + EPS) * g


def _sigmoid(x):
    return 1.0 / (1.0 + jnp.exp(-x))


def _log_sigmoid(x):
    return jnp.minimum(x, 0.0) - jnp.log1p(jnp.exp(-jnp.abs(x)))


def _tri2(n, upper):
    r = lax.broadcasted_iota(jnp.int32, (n, 2 * n), 0)
    c = lax.broadcasted_iota(jnp.int32, (n, 2 * n), 1)
    c = jnp.where(c >= n, c - n, c)
    m = (c >= r) if upper else (c <= r)
    return jnp.where(m, 1.0, 0.0).astype(BF16)


def _cumsum_rows(tri2, g):
    hi = g.astype(BF16)
    lo = (g - hi.astype(F32)).astype(BF16)
    return _dot(tri2, jnp.concatenate([hi, lo], axis=0))


def _meta_kernel(meta_ref, gpre_ref, wcc_ref, wcx_ref, wk_ref, wv_ref, wlr_ref,
                 wg_ref, bg_ref, smeta_ref, s0_ref):
    u = _rms(meta_ref[...], gpre_ref[...]).astype(BF16)
    smeta_ref[...] = _dot(u, wcc_ref[...]) * _dot(u, wcx_ref[...])
    k = _dot(u, wk_ref[...])
    v = _dot(u, wv_ref[...]).astype(BF16)
    lr = _dot(u, wlr_ref[...]).astype(BF16)
    z = _dot(lr, wg_ref[:, :DK]) + bg_ref[:, :DK]
    g = _log_sigmoid(z) * (1.0 / GATE_NORMALIZER)
    b = _cumsum_rows(_tri2(N_META, upper=False), g)
    kdec = (k * jnp.exp(b[N_META - 1:N_META, :] - b)).astype(BF16)
    for h in range(HEADS):
        s0_ref[h] = _dot_tn(kdec[:, h * HEAD_K:(h + 1) * HEAD_K],
                            v[:, h * HEAD_V:(h + 1) * HEAD_V])


def _proj_kernel(x_ref, xp_ref, xn_ref, smeta_ref, gpre_ref, w_ref, convw_ref,
                 wg_ref, bg_ref, woc_ref,
                 qf_ref, kf_ref, qb_ref, kb_ref, v_ref, rs_ref, smb_ref, a_ref,
                 df_ref, db_ref,
                 u_scr, s_scr, y_scr):
    i = pl.program_id(1)
    last = pl.num_programs(1) - 1
    t = TILE1
    gpre = gpre_ref[...]

    u_scr[0:HALO, :] = _rms(xp_ref[0], gpre).astype(BF16)
    u_scr[HALO:HALO + t, :] = _rms(x_ref[0], gpre).astype(BF16)
    u_scr[HALO + t:, :] = _rms(xn_ref[0], gpre).astype(BF16)

    def w(col, width=CW):
        return w_ref[:, col:col + width]

    for j in range(D_MODEL // CW):
        c0 = j * CW
        u_ext = u_scr[...]
        s_scr[...] = _dot(u_ext, w(COL_CC + c0)) * _dot(u_ext, w(COL_CX + c0))
        s_scr[HALO - 1:HALO, :] = jnp.where(
            i == 0, smeta_ref[N_META - 1:N_META, c0:c0 + CW], s_scr[HALO - 1:HALO, :])
        s_scr[HALO + t:HALO + t + 1, :] = jnp.where(
            i == last, 0.0, s_scr[HALO + t:HALO + t + 1, :])
        conv = (s_scr[HALO - 1:HALO - 1 + t, :] * convw_ref[0:1, c0:c0 + CW]
                + s_scr[HALO:HALO + t, :] * convw_ref[1:2, c0:c0 + CW]
                + s_scr[HALO + 1:HALO + 1 + t, :] * convw_ref[2:3, c0:c0 + CW])
        u_main = u_scr[HALO:HALO + t, :]
        cb = _dot(u_main, w(COL_CB + c0))
        cz = _dot(u_main, w(COL_CZ + c0))
        y_scr[:, c0:c0 + CW] = (cb * conv * (cz * _sigmoid(cz))).astype(BF16)

    u_main = u_scr[HALO:HALO + t, :]
    for j in range(D_MODEL // CW):
        c0 = j * CW
        pc = _dot(y_scr[...], woc_ref[:, c0:c0 + CW])
        ma = _dot(u_main, w(COL_MA + c0))
        a_ref[0, :, c0:c0 + CW] = (_sigmoid(ma) * pc).astype(BF16)
        smb_ref[0, :, c0:c0 + CW] = _sigmoid(_dot(u_main, w(COL_MB + c0))).astype(BF16)
        r = _dot(u_main, w(COL_R + c0))
        rs_ref[0, :, c0:c0 + CW] = (r * _sigmoid(r)).astype(BF16)
        v_ref[0, :, c0:c0 + CW] = _dot(u_main, w(COL_V + c0)).astype(BF16)

    lr = _dot(u_main, w(COL_LR, LR_PAD)).astype(BF16)
    z = _dot(lr, wg_ref[...]) + bg_ref[...]
    g = _log_sigmoid(z) * (1.0 / GATE_NORMALIZER)
    q = _dot(u_main, w(COL_Q, DK)) * (HEAD_K ** -0.5)
    k = _dot(u_main, w(COL_K, DK))
    tri_f = _tri2(CHUNK, upper=False)
    tri_b = _tri2(CHUNK, upper=True)
    for c in range(t // CHUNK):
        r0 = c * CHUNK
        qc = q[r0:r0 + CHUNK, :]
        kc = k[r0:r0 + CHUNK, :]
        bf = _cumsum_rows(tri_f, g[r0:r0 + CHUNK, :DK])
        bb = _cumsum_rows(tri_b, g[r0:r0 + CHUNK, DK:])
        qf_ref[0, r0:r0 + CHUNK, :] = (qc * jnp.exp(bf)).astype(BF16)
        kf_ref[0, r0:r0 + CHUNK, :] = (kc * jnp.exp(-bf)).astype(BF16)
        qb_ref[0, r0:r0 + CHUNK, :] = (qc * jnp.exp(bb)).astype(BF16)
        kb_ref[0, r0:r0 + CHUNK, :] = (kc * jnp.exp(-bb)).astype(BF16)
        df_ref[0, 0, c:c + 1, :] = jnp.exp(bf[CHUNK - 1:CHUNK, :])
        db_ref[0, 0, c:c + 1, :] = jnp.exp(bb[0:1, :])


def _gla_kernel(qf_ref, kf_ref, qb_ref, kb_ref, v_ref, rs_ref, smb_ref, a_ref, x_ref,
                dft_ref, dbt_ref, s0_ref, glag_ref, wog_ref, wo_ref, gpost_ref,
                out_ref, state_scr, ob_scr, o_scr):
    p = pl.program_id(1)
    i = pl.program_id(2)
    nt = pl.num_programs(2)
    t = TILE2
    cpt = t // CHUNK

    row = lax.broadcasted_iota(jnp.int32, (CHUNK, CHUNK), 0)
    col = lax.broadcasted_iota(jnp.int32, (CHUNK, CHUNK), 1)

    def chunk_head(q_ref, k_ref, dt_ref, mask, c, h):
        r0 = c * CHUNK
        q = q_ref[0, r0:r0 + CHUNK, h * HEAD_K:(h + 1) * HEAD_K]
        k = k_ref[0, r0:r0 + CHUNK, h * HEAD_K:(h + 1) * HEAD_K]
        v = v_ref[0, r0:r0 + CHUNK, h * HEAD_V:(h + 1) * HEAD_V]
        sc = jnp.where(mask, _dot_nt(q, k), 0.0).astype(BF16)
        state = state_scr[h]
        o = _dot(sc, v) + _dot(q, state.astype(BF16))
        d = dt_ref[0, 0, h * HEAD_K:(h + 1) * HEAD_K, c:c + 1]
        state_scr[h] = (state + _dot_tn(k, v)) * d
        return o

    @pl.when(jnp.logical_and(p == 0, i == 0))
    def _():
        state_scr[...] = jnp.zeros_like(state_scr)

    @pl.when(jnp.logical_and(p == 1, i == 0))
    def _():
        state_scr[...] = s0_ref[...]

    @pl.when(p == 0)
    def _():
        base = (nt - 1 - i) * t
        for c in reversed(range(cpt)):
            for h in range(HEADS):
                o = chunk_head(qb_ref, kb_ref, dbt_ref, col > row, c, h)
                ob_scr[pl.ds(pl.multiple_of(base + c * CHUNK, CHUNK), CHUNK),
                       h * HEAD_V:(h + 1) * HEAD_V] = o

    @pl.when(p == 1)
    def _():
        base = i * t
        for c in range(cpt):
            for h in range(HEADS):
                o = chunk_head(qf_ref, kf_ref, dft_ref, col <= row, c, h)
                ob = ob_scr[pl.ds(pl.multiple_of(base + c * CHUNK, CHUNK), CHUNK),
                            h * HEAD_V:(h + 1) * HEAD_V]
                o_scr[c * CHUNK:(c + 1) * CHUNK, h * HEAD_V:(h + 1) * HEAD_V] = o + ob
        glag = glag_ref[...]
        ys = []
        for h in range(HEADS):
            hs = slice(h * HEAD_V, (h + 1) * HEAD_V)
            oh = _rms(o_scr[:, hs], glag)
            ys.append((oh * rs_ref[0, :, hs].astype(F32)).astype(BF16))
        y = jnp.concatenate(ys, axis=-1)
        p_gla = _dot(y, wog_ref[...])
        merged = a_ref[0].astype(F32) + smb_ref[0].astype(F32) * p_gla
        out = _dot(merged.astype(BF16), wo_ref[...])
        out_ref[0] = x_ref[0] + _rms(out, gpost_ref[...])


def kernel(x, meta_tokens, norm_pre, w_in, conv_w, w_gate_fwd, b_gate_fwd, w_gate_bwd,
           b_gate_bwd, gla_norm, w_out_conv, w_out_gla, w_merge_out, norm_post):
    bsz, seq, _ = x.shape
    assert seq % TILE1 == 0 and seq % TILE2 == 0 and norm_pre.shape[0] == 1
    nt1 = seq // TILE1
    nt2 = seq // TILE2
    cpt1 = TILE1 // CHUNK
    cpt2 = TILE2 // CHUNK
    n_chunks = seq // CHUNK

    w = w_in[0]
    lr0 = 7168
    w_r = jnp.concatenate(
        [w[:, :lr0], w[:, lr0 + 2 * GATE_RANK:], w[:, lr0:lr0 + 2 * GATE_RANK],
         jnp.zeros((D_MODEL, LR_PAD - 2 * GATE_RANK), w.dtype)], axis=1).astype(BF16)
    wg = jnp.zeros((LR_PAD, 2 * DK), F32)
    wg = wg.at[:GATE_RANK, :DK].set(w_gate_fwd[0])
    wg = wg.at[GATE_RANK:2 * GATE_RANK, DK:].set(w_gate_bwd[0]).astype(BF16)
    bg = jnp.concatenate([b_gate_fwd[0], b_gate_bwd[0]])[None, :]
    woc = w_out_conv[0].astype(BF16)
    wog = w_out_gla[0].astype(BF16)
    wo = w_merge_out[0].astype(BF16)
    gpre = norm_pre[0][None, :]
    gpost = norm_post[0][None, :]
    glag = gla_norm[0][None, :]
    convw = conv_w[0]

    def col_spec(width, col):
        return pl.BlockSpec((D_MODEL, width), lambda g, c=col // width: (0, c))

    def full1(shape):
        return pl.BlockSpec(shape, lambda g: (0,) * len(shape))

    smeta, s0 = pl.pallas_call(
        _meta_kernel,
        grid=(1,),
        in_specs=[full1((N_META, D_MODEL)), full1((1, D_MODEL)),
                  col_spec(1024, COL_CC), col_spec(1024, COL_CX),
                  col_spec(DK, COL_K), col_spec(DV, COL_V), col_spec(LR_PAD, COL_LR),
                  full1((LR_PAD, 2 * DK)), full1((1, 2 * DK))],
        out_specs=[full1((N_META, D_MODEL)), full1((HEADS, HEAD_K, HEAD_V))],
        out_shape=[jax.ShapeDtypeStruct((N_META, D_MODEL), F32),
                   jax.ShapeDtypeStruct((HEADS, HEAD_K, HEAD_V), F32)],
        compiler_params=pltpu.CompilerParams(vmem_limit_bytes=VMEM_LIMIT),
        name="meta_prologue",
    )(meta_tokens, gpre, w_r, w_r, w_r, w_r, w_r, wg, bg)

    hb = TILE1 // HALO
    n_hb = seq // HALO

    def const2(shape):
        return pl.BlockSpec(shape, lambda b, i: (0,) * len(shape),
                            pipeline_mode=pl.Buffered(1))

    def tok_spec(width):
        return pl.BlockSpec((1, TILE1, width), lambda b, i: (b, i, 0))

    d_spec = pl.BlockSpec((1, 1, cpt1, DK), lambda b, i: (b, i, 0, 0))
    tok_shape = lambda width: jax.ShapeDtypeStruct((bsz, seq, width), BF16)
    d_shape = jax.ShapeDtypeStruct((bsz, nt1, cpt1, DK), F32)

    qf, kf, qb, kb, v, rs, smb, a, df, db = pl.pallas_call(
        _proj_kernel,
        grid=(bsz, nt1),
        in_specs=[
            tok_spec(D_MODEL),
            pl.BlockSpec((1, HALO, D_MODEL),
                         lambda b, i: (b, jnp.maximum(i * hb - 1, 0), 0)),
            pl.BlockSpec((1, HALO, D_MODEL),
                         lambda b, i: (b, jnp.minimum((i + 1) * hb, n_hb - 1), 0)),
            const2((N_META, D_MODEL)), const2((1, D_MODEL)), const2((D_MODEL, N_COLS)),
            const2((3, D_MODEL)), const2((LR_PAD, 2 * DK)), const2((1, 2 * DK)),
            const2((D_MODEL, D_MODEL)),
        ],
        out_specs=[tok_spec(DK), tok_spec(DK), tok_spec(DK), tok_spec(DK),
                   tok_spec(DV), tok_spec(DV), tok_spec(D_MODEL), tok_spec(D_MODEL),
                   d_spec, d_spec],
        out_shape=[tok_shape(DK), tok_shape(DK), tok_shape(DK), tok_shape(DK),
                   tok_shape(DV), tok_shape(DV), tok_shape(D_MODEL), tok_shape(D_MODEL),
                   d_shape, d_shape],
        scratch_shapes=[pltpu.VMEM((TILE1 + 2 * HALO, D_MODEL), BF16),
                        pltpu.VMEM((TILE1 + 2 * HALO, CW), F32),
                        pltpu.VMEM((TILE1, D_MODEL), BF16)],
        compiler_params=pltpu.CompilerParams(
            dimension_semantics=("parallel", "arbitrary"),
            vmem_limit_bytes=VMEM_LIMIT),
        name="inproj_conv_gates",
    )(x, x, x, smeta, gpre, w_r, convw, wg, bg, woc)

    def to_cols(d):
        return d.reshape(bsz, nt2, cpt2, DK).transpose(0, 1, 3, 2)

    dft = to_cols(df)
    dbt = to_cols(db)

    def fwd_map(b, p, i):
        return (b, p * i, 0)

    def bwd_map(b, p, i):
        return (b, (1 - p) * (nt2 - 1 - i), 0)

    def both_map(b, p, i):
        return (b, p * i + (1 - p) * (nt2 - 1 - i), 0)

    def tok2(width, imap):
        return pl.BlockSpec((1, TILE2, width), imap)

    def const3(shape):
        return pl.BlockSpec(shape, lambda b, p, i: (0,) * len(shape))

    dt_f_spec = pl.BlockSpec((1, 1, DK, cpt2), lambda b, p, i: (b, p * i, 0, 0))
    dt_b_spec = pl.BlockSpec((1, 1, DK, cpt2),
                             lambda b, p, i: (b, (1 - p) * (nt2 - 1 - i), 0, 0))

    out = pl.pallas_call(
        _gla_kernel,
        grid=(bsz, 2, nt2),
        in_specs=[
            tok2(DK, fwd_map), tok2(DK, fwd_map), tok2(DK, bwd_map), tok2(DK, bwd_map),
            tok2(DV, both_map), tok2(DV, fwd_map), tok2(D_MODEL, fwd_map),
            tok2(D_MODEL, fwd_map), tok2(D_MODEL, fwd_map),
            dt_f_spec, dt_b_spec,
            const3((HEADS, HEAD_K, HEAD_V)), const3((1, HEAD_V)),
            const3((DV, D_MODEL)), const3((D_MODEL, D_MODEL)), const3((1, D_MODEL)),
        ],
        out_specs=tok2(D_MODEL, fwd_map),
        out_shape=jax.ShapeDtypeStruct((bsz, seq, D_MODEL), x.dtype),
        scratch_shapes=[pltpu.VMEM((HEADS, HEAD_K, HEAD_V), F32),
                        pltpu.VMEM((seq, DV), F32),
                        pltpu.VMEM((TILE2, DV), F32)],
        compiler_params=pltpu.CompilerParams(
            dimension_semantics=("parallel", "arbitrary", "arbitrary"),
            vmem_limit_bytes=VMEM_LIMIT),
        name="gla_merge_out",
    )(qf, kf, qb, kb, v, rs, smb, a, x, dft, dbt, s0, glag, wog, wo, gpost)
    return out
```

```python
import functools

import jax
import jax.numpy as jnp
from jax import lax
from jax.experimental import pallas as pl
from jax.experimental.pallas import tpu as pltpu

D_MODEL = 1024
N_META = 16
HEADS = 4
DK = 512
DV = 1024
HEAD_K = DK // HEADS
HEAD_V = DV // HEADS
GATE_RANK = 16
GATE_NORMALIZER = 16.0
CHUNK = 64
EPS = 1e-6

COL_CB, COL_CC, COL_CX, COL_CZ = 0, 1024, 2048, 3072
COL_Q, COL_K, COL_V, COL_R = 4096, 4608, 5120, 6144
COL_MA, COL_MB, COL_LR = 7168, 8192, 9216
LR_PAD = 128
N_COLS = COL_LR + LR_PAD

HALO = 16
TILE1 = 512
TILE2 = 512
CW = 256
VMEM_LIMIT = 56 * 1024 * 1024

F32 = jnp.float32
BF16 = jnp.bfloat16


def _dot(a, b):
    return jnp.dot(a, b, preferred_element_type=F32)


def _dot_nt(a, b):
    return lax.dot_general(a, b, (((1,), (1,)), ((), ())), preferred_element_type=F32)


def _dot_tn(a, b):
    return lax.dot_general(a, b, (((0,), (0,)), ((), ())), preferred_element_type=F32)


def _rms(x, g):
    ms = jnp.mean(x * x, axis=-1, keepdims=True)
    return x * lax.rsqrt(ms + EPS) * g


def _sigmoid(x):
    return 1.0 / (1.0 + jnp.exp(-x))


def _log_sigmoid(x):
    return jnp.minimum(x, 0.0) - jnp.log1p(jnp.exp(-jnp.abs(x)))


def _tri2(n, upper):
    r = lax.broadcasted_iota(jnp.int32, (n, 2 * n), 0)
    c = lax.broadcasted_iota(jnp.int32, (n, 2 * n), 1)
    c = jnp.where(c >= n, c - n, c)
    m = (c >= r) if upper else (c <= r)
    return jnp.where(m, 1.0, 0.0).astype(BF16)


def _cumsum_rows(tri2, g):
    hi = g.astype(BF16)
    lo = (g - hi.astype(F32)).astype(BF16)
    return _dot(tri2, jnp.concatenate([hi, lo], axis=0))


def _meta_kernel(meta_ref, gpre_ref, wcc_ref, wcx_ref, wk_ref, wv_ref, wlr_ref,
                 wg_ref, bg_ref, smeta_ref, s0_ref):
    u = _rms(meta_ref[...], gpre_ref[...]).astype(BF16)
    smeta_ref[...] = _dot(u, wcc_ref[...]) * _dot(u, wcx_ref[...])
    k = _dot(u, wk_ref[...])
    v = _dot(u, wv_ref[...]).astype(BF16)
    lr = _dot(u, wlr_ref[...]).astype(BF16)
    z = _dot(lr, wg_ref[:, :DK]) + bg_ref[:, :DK]
    g = _log_sigmoid(z) * (1.0 / GATE_NORMALIZER)
    b = _cumsum_rows(_tri2(N_META, upper=False), g)
    kdec = (k * jnp.exp(b[N_META - 1:N_META, :] - b)).astype(BF16)
    for h in range(HEADS):
        s0_ref[h] = _dot_tn(kdec[:, h * HEAD_K:(h + 1) * HEAD_K],
                            v[:, h * HEAD_V:(h + 1) * HEAD_V])


def _proj_kernel(x_ref, xp_ref, xn_ref, smeta_ref, gpre_ref, w_ref, convw_ref,
                 wg_ref, bg_ref, woc_ref,
                 qf_ref, kf_ref, qb_ref, kb_ref, v_ref, rs_ref, smb_ref, a_ref,
                 df_ref, db_ref,
                 u_scr, s_scr, y_scr):
    i = pl.program_id(1)
    last = pl.num_programs(1) - 1
    t = TILE1
    gpre = gpre_ref[...]

    u_scr[0:HALO, :] = _rms(xp_ref[0], gpre).astype(BF16)
    u_scr[HALO:HALO + t, :] = _rms(x_ref[0], gpre).astype(BF16)
    u_scr[HALO + t:, :] = _rms(xn_ref[0], gpre).astype(BF16)

    def w(col, width=CW):
        return w_ref[:, col:col + width]

    for j in range(D_MODEL // CW):
        c0 = j * CW
        u_ext = u_scr[...]
        s_scr[...] = _dot(u_ext, w(COL_CC + c0)) * _dot(u_ext, w(COL_CX + c0))
        s_scr[HALO - 1:HALO, :] = jnp.where(
            i == 0, smeta_ref[N_META - 1:N_META, c0:c0 + CW], s_scr[HALO - 1:HALO, :])
        s_scr[HALO + t:HALO + t + 1, :] = jnp.where(
            i == last, 0.0, s_scr[HALO + t:HALO + t + 1, :])
        conv = (s_scr[HALO - 1:HALO - 1 + t, :] * convw_ref[0:1, c0:c0 + CW]
                + s_scr[HALO:HALO + t, :] * convw_ref[1:2, c0:c0 + CW]
                + s_scr[HALO + 1:HALO + 1 + t, :] * convw_ref[2:3, c0:c0 + CW])
        u_main = u_scr[HALO:HALO + t, :]
        cb = _dot(u_main, w(COL_CB + c0))
        cz = _dot(u_main, w(COL_CZ + c0))
        y_scr[:, c0:c0 + CW] = (cb * conv * (cz * _sigmoid(cz))).astype(BF16)

    u_main = u_scr[HALO:HALO + t, :]
    for j in range(D_MODEL // CW):
        c0 = j * CW
        pc = _dot(y_scr[...], woc_ref[:, c0:c0 + CW])
        ma = _dot(u_main, w(COL_MA + c0))
        a_ref[0, :, c0:c0 + CW] = (_sigmoid(ma) * pc).astype(BF16)
        smb_ref[0, :, c0:c0 + CW] = _sigmoid(_dot(u_main, w(COL_MB + c0))).astype(BF16)
        r = _dot(u_main, w(COL_R + c0))
        rs_ref[0, :, c0:c0 + CW] = (r * _sigmoid(r)).astype(BF16)
        v_ref[0, :, c0:c0 + CW] = _dot(u_main, w(COL_V + c0)).astype(BF16)

    lr = _dot(u_main, w(COL_LR, LR_PAD)).astype(BF16)
    z = _dot(lr, wg_ref[...]) + bg_ref[...]
    g = _log_sigmoid(z) * (1.0 / GATE_NORMALIZER)
    q = _dot(u_main, w(COL_Q, DK)) * (HEAD_K ** -0.5)
    k = _dot(u_main, w(COL_K, DK))
    tri_f = _tri2(CHUNK, upper=False)
    tri_b = _tri2(CHUNK, upper=True)
    for c in range(t // CHUNK):
        r0 = c * CHUNK
        qc = q[r0:r0 + CHUNK, :]
        kc = k[r0:r0 + CHUNK, :]
        bf = _cumsum_rows(tri_f, g[r0:r0 + CHUNK, :DK])
        bb = _cumsum_rows(tri_b, g[r0:r0 + CHUNK, DK:])
        qf_ref[0, r0:r0 + CHUNK, :] = (qc * jnp.exp(bf)).astype(BF16)
        kf_ref[0, r0:r0 + CHUNK, :] = (kc * jnp.exp(-bf)).astype(BF16)
        qb_ref[0, r0:r0 + CHUNK, :] = (qc * jnp.exp(bb)).astype(BF16)
        kb_ref[0, r0:r0 + CHUNK, :] = (kc * jnp.exp(-bb)).astype(BF16)
        df_ref[0, 0, c:c + 1, :] = jnp.exp(bf[CHUNK - 1:CHUNK, :])
        db_ref[0, 0, c:c + 1, :] = jnp.exp(bb[0:1, :])


def _gla_kernel(qf_ref, kf_ref, qb_ref, kb_ref, v_ref, rs_ref, smb_ref, a_ref, x_ref,
                dft_ref, dbt_ref, s0_ref, glag_ref, wog_ref, wo_ref, gpost_ref,
                out_ref, state_scr, ob_scr, o_scr, sc_scr, kv_scr, sb_scr):
    p = pl.program_id(1)
    i = pl.program_id(2)
    nt = pl.num_programs(2)
    t = TILE2
    cpt = t // CHUNK

    row = lax.broadcasted_iota(jnp.int32, (CHUNK, CHUNK), 0)
    col = lax.broadcasted_iota(jnp.int32, (CHUNK, CHUNK), 1)

    def rows(c):
        return slice(c * CHUNK, (c + 1) * CHUNK)

    def hk(h):
        return slice(h * HEAD_K, (h + 1) * HEAD_K)

    def hv(h):
        return slice(h * HEAD_V, (h + 1) * HEAD_V)

    def scan_tile(q_ref, k_ref, dt_ref, mask, order, emit):
        for c in range(cpt):
            for h in range(HEADS):
                q = q_ref[0, rows(c), hk(h)]
                k = k_ref[0, rows(c), hk(h)]
                sc_scr[c, h] = jnp.where(mask, _dot_nt(q, k), 0.0).astype(BF16)
                kv_scr[c, h] = _dot_tn(k, v_ref[0, rows(c), hv(h)])
        for h in range(HEADS):
            state = state_scr[h]
            for c in order:
                sb_scr[c, h] = state.astype(BF16)
                state = (state + kv_scr[c, h]) * dt_ref[0, 0, hk(h), c:c + 1]
            state_scr[h] = state
        for c in range(cpt):
            for h in range(HEADS):
                lhs = jnp.concatenate([q_ref[0, rows(c), hk(h)], sc_scr[c, h]], axis=1)
                rhs = jnp.concatenate([sb_scr[c, h], v_ref[0, rows(c), hv(h)]], axis=0)
                emit(c, h, _dot(lhs, rhs))

    @pl.when(jnp.logical_and(p == 0, i == 0))
    def _():
        state_scr[...] = jnp.zeros_like(state_scr)

    @pl.when(jnp.logical_and(p == 1, i == 0))
    def _():
        state_scr[...] = s0_ref[...]

    @pl.when(p == 0)
    def _():
        base = (nt - 1 - i) * t

        def emit(c, h, o):
            ob_scr[pl.ds(pl.multiple_of(base + c * CHUNK, CHUNK), CHUNK), hv(h)] = o

        scan_tile(qb_ref, kb_ref, dbt_ref, col > row, list(reversed(range(cpt))), emit)

    @pl.when(p == 1)
    def _():
        base = i * t

        def emit(c, h, o):
            ob = ob_scr[pl.ds(pl.multiple_of(base + c * CHUNK, CHUNK), CHUNK), hv(h)]
            o_scr[rows(c), hv(h)] = o + ob

        scan_tile(qf_ref, kf_ref, dft_ref, col <= row, list(range(cpt)), emit)
        glag = glag_ref[...]
        ys = []
        for h in range(HEADS):
            oh = _rms(o_scr[:, hv(h)], glag)
            ys.append((oh * rs_ref[0, :, hv(h)].astype(F32)).astype(BF16))
        y = jnp.concatenate(ys, axis=-1)
        p_gla = _dot(y, wog_ref[...])
        merged = a_ref[0].astype(F32) + smb_ref[0].astype(F32) * p_gla
        out = _dot(merged.astype(BF16), wo_ref[...])
        out_ref[0] = x_ref[0] + _rms(out, gpost_ref[...])


def kernel(x, meta_tokens, norm_pre, w_in, conv_w, w_gate_fwd, b_gate_fwd, w_gate_bwd,
           b_gate_bwd, gla_norm, w_out_conv, w_out_gla, w_merge_out, norm_post):
    bsz, seq, _ = x.shape
    assert seq % TILE1 == 0 and seq % TILE2 == 0 and norm_pre.shape[0] == 1
    nt1 = seq // TILE1
    nt2 = seq // TILE2
    cpt1 = TILE1 // CHUNK
    cpt2 = TILE2 // CHUNK
    n_chunks = seq // CHUNK

    w = w_in[0]
    lr0 = 7168
    w_r = jnp.concatenate(
        [w[:, :lr0], w[:, lr0 + 2 * GATE_RANK:], w[:, lr0:lr0 + 2 * GATE_RANK],
         jnp.zeros((D_MODEL, LR_PAD - 2 * GATE_RANK), w.dtype)], axis=1).astype(BF16)
    wg = jnp.zeros((LR_PAD, 2 * DK), F32)
    wg = wg.at[:GATE_RANK, :DK].set(w_gate_fwd[0])
    wg = wg.at[GATE_RANK:2 * GATE_RANK, DK:].set(w_gate_bwd[0]).astype(BF16)
    bg = jnp.concatenate([b_gate_fwd[0], b_gate_bwd[0]])[None, :]
    woc = w_out_conv[0].astype(BF16)
    wog = w_out_gla[0].astype(BF16)
    wo = w_merge_out[0].astype(BF16)
    gpre = norm_pre[0][None, :]
    gpost = norm_post[0][None, :]
    glag = gla_norm[0][None, :]
    convw = conv_w[0]

    def col_spec(width, col):
        return pl.BlockSpec((D_MODEL, width), lambda g, c=col // width: (0, c))

    def full1(shape):
        return pl.BlockSpec(shape, lambda g: (0,) * len(shape))

    smeta, s0 = pl.pallas_call(
        _meta_kernel,
        grid=(1,),
        in_specs=[full1((N_META, D_MODEL)), full1((1, D_MODEL)),
                  col_spec(1024, COL_CC), col_spec(1024, COL_CX),
                  col_spec(DK, COL_K), col_spec(DV, COL_V), col_spec(LR_PAD, COL_LR),
                  full1((LR_PAD, 2 * DK)), full1((1, 2 * DK))],
        out_specs=[full1((N_META, D_MODEL)), full1((HEADS, HEAD_K, HEAD_V))],
        out_shape=[jax.ShapeDtypeStruct((N_META, D_MODEL), F32),
                   jax.ShapeDtypeStruct((HEADS, HEAD_K, HEAD_V), F32)],
        compiler_params=pltpu.CompilerParams(vmem_limit_bytes=VMEM_LIMIT),
        name="meta_prologue",
    )(meta_tokens, gpre, w_r, w_r, w_r, w_r, w_r, wg, bg)

    hb = TILE1 // HALO
    n_hb = seq // HALO

    def const2(shape):
        return pl.BlockSpec(shape, lambda b, i: (0,) * len(shape),
                            pipeline_mode=pl.Buffered(1))

    def tok_spec(width):
        return pl.BlockSpec((1, TILE1, width), lambda b, i: (b, i, 0))

    d_spec = pl.BlockSpec((1, 1, cpt1, DK), lambda b, i: (b, i, 0, 0))
    tok_shape = lambda width: jax.ShapeDtypeStruct((bsz, seq, width), BF16)
    d_shape = jax.ShapeDtypeStruct((bsz, nt1, cpt1, DK), F32)

    qf, kf, qb, kb, v, rs, smb, a, df, db = pl.pallas_call(
        _proj_kernel,
        grid=(bsz, nt1),
        in_specs=[
            tok_spec(D_MODEL),
            pl.BlockSpec((1, HALO, D_MODEL),
                         lambda b, i: (b, jnp.maximum(i * hb - 1, 0), 0)),
            pl.BlockSpec((1, HALO, D_MODEL),
                         lambda b, i: (b, jnp.minimum((i + 1) * hb, n_hb - 1), 0)),
            const2((N_META, D_MODEL)), const2((1, D_MODEL)), const2((D_MODEL, N_COLS)),
            const2((3, D_MODEL)), const2((LR_PAD, 2 * DK)), const2((1, 2 * DK)),
            const2((D_MODEL, D_MODEL)),
        ],
        out_specs=[tok_spec(DK), tok_spec(DK), tok_spec(DK), tok_spec(DK),
                   tok_spec(DV), tok_spec(DV), tok_spec(D_MODEL), tok_spec(D_MODEL),
                   d_spec, d_spec],
        out_shape=[tok_shape(DK), tok_shape(DK), tok_shape(DK), tok_shape(DK),
                   tok_shape(DV), tok_shape(DV), tok_shape(D_MODEL), tok_shape(D_MODEL),
                   d_shape, d_shape],
        scratch_shapes=[pltpu.VMEM((TILE1 + 2 * HALO, D_MODEL), BF16),
                        pltpu.VMEM((TILE1 + 2 * HALO, CW), F32),
                        pltpu.VMEM((TILE1, D_MODEL), BF16)],
        compiler_params=pltpu.CompilerParams(
            dimension_semantics=("parallel", "arbitrary"),
            vmem_limit_bytes=VMEM_LIMIT),
        name="inproj_conv_gates",
    )(x, x, x, smeta, gpre, w_r, convw, wg, bg, woc)

    def to_cols(d):
        return d.reshape(bsz, nt2, cpt2, DK).transpose(0, 1, 3, 2)

    dft = to_cols(df)
    dbt = to_cols(db)

    def fwd_map(b, p, i):
        return (b, p * i, 0)

    def bwd_map(b, p, i):
        return (b, (1 - p) * (nt2 - 1 - i), 0)

    def both_map(b, p, i):
        return (b, p * i + (1 - p) * (nt2 - 1 - i), 0)

    def tok2(width, imap):
        return pl.BlockSpec((1, TILE2, width), imap)

    def const3(shape):
        return pl.BlockSpec(shape, lambda b, p, i: (0,) * len(shape))

    dt_f_spec = pl.BlockSpec((1, 1, DK, cpt2), lambda b, p, i: (b, p * i, 0, 0))
    dt_b_spec = pl.BlockSpec((1, 1, DK, cpt2),
                             lambda b, p, i: (b, (1 - p) * (nt2 - 1 - i), 0, 0))

    out = pl.pallas_call(
        _gla_kernel,
        grid=(bsz, 2, nt2),
        in_specs=[
            tok2(DK, fwd_map), tok2(DK, fwd_map), tok2(DK, bwd_map), tok2(DK, bwd_map),
            tok2(DV, both_map), tok2(DV, fwd_map), tok2(D_MODEL, fwd_map),
            tok2(D_MODEL, fwd_map), tok2(D_MODEL, fwd_map),
            dt_f_spec, dt_b_spec,
            const3((HEADS, HEAD_K, HEAD_V)), const3((1, HEAD_V)),
            const3((DV, D_MODEL)), const3((D_MODEL, D_MODEL)), const3((1, D_MODEL)),
        ],
        out_specs=tok2(D_MODEL, fwd_map),
        out_shape=jax.ShapeDtypeStruct((bsz, seq, D_MODEL), x.dtype),
        scratch_shapes=[pltpu.VMEM((HEADS, HEAD_K, HEAD_V), F32),
                        pltpu.VMEM((seq, DV), F32),
                        pltpu.VMEM((TILE2, DV), F32),
                        pltpu.VMEM((cpt2, HEADS, CHUNK, CHUNK), BF16),
                        pltpu.VMEM((cpt2, HEADS, HEAD_K, HEAD_V), F32),
                        pltpu.VMEM((cpt2, HEADS, HEAD_K, HEAD_V), BF16)],
        compiler_params=pltpu.CompilerParams(
            dimension_semantics=("parallel", "arbitrary", "arbitrary"),
            vmem_limit_bytes=VMEM_LIMIT),
        name="gla_merge_out",
    )(qf, kf, qb, kb, v, rs, smb, a, x, dft, dbt, s0, glag, wog, wo, gpost)
    return out
```

```python
import jax
import jax.numpy as jnp
from jax import lax
from jax.experimental import pallas as pl
from jax.experimental.pallas import tpu as pltpu

D_MODEL = 1024
N_META = 16
HEADS = 4
DK = 512
DV = 1024
HEAD_K = DK // HEADS
HEAD_V = DV // HEADS
GATE_RANK = 16
GATE_NORMALIZER = 16.0
EPS = 1e-6

COL_CB, COL_CC, COL_CX, COL_CZ = 0, 1024, 2048, 3072
COL_Q, COL_K, COL_V, COL_R = 4096, 4608, 5120, 6144
COL_LR = 7168
N_MAIN = COL_LR
COL_MERGE = COL_LR + 2 * GATE_RANK
LR_PAD = 128

HALO = 16
TILE = 512
BLOCK = 128
BPT = TILE // BLOCK
CW = 256
VMEM_LIMIT = 56 * 1024 * 1024

F32 = jnp.float32
BF16 = jnp.bfloat16


def _dot(a, b):
    return jnp.dot(a, b, preferred_element_type=F32)


def _dot_nt(a, b):
    return lax.dot_general(a, b, (((1,), (1,)), ((), ())), preferred_element_type=F32)


def _dot_tn(a, b):
    return lax.dot_general(a, b, (((0,), (0,)), ((), ())), preferred_element_type=F32)


def _rms(x, g):
    ms = jnp.mean(x * x, axis=-1, keepdims=True)
    return x * lax.rsqrt(ms + EPS) * g


def _sigmoid(x):
    return 1.0 / (1.0 + jnp.exp(-x))


def _log_sigmoid(x):
    return jnp.minimum(x, 0.0) - jnp.log1p(jnp.exp(-jnp.abs(x)))


def _sum_matrix(n, ref, reverse):
    r = lax.broadcasted_iota(jnp.int32, (n, 2 * n), 0)
    c = lax.broadcasted_iota(jnp.int32, (n, 2 * n), 1)
    c = jnp.where(c >= n, c - n, c)
    if reverse:
        m = jnp.where(c >= r, 1.0, 0.0) - jnp.where(c >= ref, 1.0, 0.0)
    else:
        m = jnp.where(c <= r, 1.0, 0.0) - jnp.where(c < ref, 1.0, 0.0)
    return m.astype(BF16)


def _row_sums(mat2, g):
    hi = g.astype(BF16)
    lo = (g - hi.astype(F32)).astype(BF16)
    return _dot(mat2, jnp.concatenate([hi, lo], axis=0))


def _meta_kernel(meta_ref, gpre_ref, wcc_ref, wcx_ref, wk_ref, wv_ref, wlr_ref,
                 wg_ref, bg_ref, smeta_ref, s0_ref):
    u = _rms(meta_ref[...], gpre_ref[...]).astype(BF16)
    smeta_ref[...] = _dot(u, wcc_ref[...]) * _dot(u, wcx_ref[...])
    k = _dot(u, wk_ref[...])
    v = _dot(u, wv_ref[...]).astype(BF16)
    lr = _dot(u, wlr_ref[...]).astype(BF16)
    z = _dot(lr, wg_ref[:, :DK]) + bg_ref[:, :DK]
    g = _log_sigmoid(z) * (1.0 / GATE_NORMALIZER)
    b = _row_sums(_sum_matrix(N_META, 0, reverse=False), g)
    kdec = (k * jnp.exp(b[N_META - 1:N_META, :] - b)).astype(BF16)
    for h in range(HEADS):
        s0_ref[h] = _dot_tn(kdec[:, h * HEAD_K:(h + 1) * HEAD_K],
                            v[:, h * HEAD_V:(h + 1) * HEAD_V])


def _proj_kernel(x_ref, xp_ref, xn_ref, smeta_ref, gpre_ref, w_ref, wm_ref, wlr_ref,
                 convw_ref, wg_ref, bg_ref, woc_ref,
                 qf_ref, kf_ref, qb_ref, kb_ref, v_ref, rs_ref, smb_ref, a_ref, d_ref,
                 u_scr, s_scr, y_scr):
    i = pl.program_id(1)
    last = pl.num_programs(1) - 1
    t = TILE
    gpre = gpre_ref[...]

    u_scr[0:HALO, :] = _rms(xp_ref[0], gpre).astype(BF16)
    u_scr[HALO:HALO + t, :] = _rms(x_ref[0], gpre).astype(BF16)
    u_scr[HALO + t:, :] = _rms(xn_ref[0], gpre).astype(BF16)

    def w(col, width=CW):
        return w_ref[:, col:col + width]

    for j in range(D_MODEL // CW):
        c0 = j * CW
        u_ext = u_scr[...]
        s_scr[...] = _dot(u_ext, w(COL_CC + c0)) * _dot(u_ext, w(COL_CX + c0))
        s_scr[HALO - 1:HALO, :] = jnp.where(
            i == 0, smeta_ref[N_META - 1:N_META, c0:c0 + CW], s_scr[HALO - 1:HALO, :])
        s_scr[HALO + t:HALO + t + 1, :] = jnp.where(
            i == last, 0.0, s_scr[HALO + t:HALO + t + 1, :])
        conv = (s_scr[HALO - 1:HALO - 1 + t, :] * convw_ref[0:1, c0:c0 + CW]
                + s_scr[HALO:HALO + t, :] * convw_ref[1:2, c0:c0 + CW]
                + s_scr[HALO + 1:HALO + 1 + t, :] * convw_ref[2:3, c0:c0 + CW])
        u_main = u_scr[HALO:HALO + t, :]
        cb = _dot(u_main, w(COL_CB + c0))
        cz = _dot(u_main, w(COL_CZ + c0))
        y_scr[:, c0:c0 + CW] = (cb * conv * (cz * _sigmoid(cz))).astype(BF16)

    u_main = u_scr[HALO:HALO + t, :]
    for j in range(D_MODEL // CW):
        c0 = j * CW
        pc = _dot(y_scr[...], woc_ref[:, c0:c0 + CW])
        ma = _dot(u_main, wm_ref[:, c0:c0 + CW])
        a_ref[0, :, c0:c0 + CW] = (_sigmoid(ma) * pc).astype(BF16)
        mb = _dot(u_main, wm_ref[:, D_MODEL + c0:D_MODEL + c0 + CW])
        smb_ref[0, :, c0:c0 + CW] = _sigmoid(mb).astype(BF16)
        r = _dot(u_main, w(COL_R + c0))
        rs_ref[0, :, c0:c0 + CW] = (r * _sigmoid(r)).astype(BF16)
        v_ref[0, :, c0:c0 + CW] = _dot(u_main, w(COL_V + c0)).astype(BF16)

    lr = _dot(u_main, wlr_ref[...]).astype(BF16)
    z = _dot(lr, wg_ref[...]) + bg_ref[...]
    g = _log_sigmoid(z) * (1.0 / GATE_NORMALIZER)
    q = _dot(u_main, w(COL_Q, DK)) * (HEAD_K ** -0.5)
    k = _dot(u_main, w(COL_K, DK))
    half = BLOCK // 2
    mat_f = _sum_matrix(BLOCK, half, reverse=False)
    mat_b = _sum_matrix(BLOCK, half, reverse=True)
    for c in range(BPT):
        rs = slice(c * BLOCK, (c + 1) * BLOCK)
        qc = q[rs, :]
        kc = k[rs, :]
        gf = g[rs, :DK]
        gb = g[rs, DK:]
        bf = _row_sums(mat_f, gf)
        bb = _row_sums(mat_b, gb)
        qf_ref[0, rs, :] = (qc * jnp.exp(bf)).astype(BF16)
        kf_ref[0, rs, :] = (kc * jnp.exp(-bf)).astype(BF16)
        qb_ref[0, rs, :] = (qc * jnp.exp(bb)).astype(BF16)
        kb_ref[0, rs, :] = (kc * jnp.exp(-bb)).astype(BF16)
        d_ref[0, 0, c:c + 1, :] = jnp.exp(gf[0:1, :] - bf[0:1, :])
        d_ref[0, 0, BPT + c:BPT + c + 1, :] = jnp.exp(bf[BLOCK - 1:BLOCK, :])
        d_ref[0, 0, 2 * BPT + c:2 * BPT + c + 1, :] = jnp.exp(
            gb[BLOCK - 1:BLOCK, :] - bb[BLOCK - 1:BLOCK, :])
        d_ref[0, 0, 3 * BPT + c:3 * BPT + c + 1, :] = jnp.exp(bb[0:1, :])


def _gla_kernel(qf_ref, kf_ref, qb_ref, kb_ref, v_ref, rs_ref, smb_ref, a_ref, x_ref,
                d_ref, s0_ref, glag_ref, wog_ref, wo_ref, gpost_ref,
                out_ref, state_scr, ob_scr, o_scr, sc_scr, kv_scr, sb_scr):
    p = pl.program_id(1)
    i = pl.program_id(2)
    nt = pl.num_programs(2)
    t = TILE

    row = lax.broadcasted_iota(jnp.int32, (BLOCK, BLOCK), 0)
    col = lax.broadcasted_iota(jnp.int32, (BLOCK, BLOCK), 1)
    eye = row == col

    def rows(c):
        return slice(c * BLOCK, (c + 1) * BLOCK)

    def hk(h):
        return slice(h * HEAD_K, (h + 1) * HEAD_K)

    def hv(h):
        return slice(h * HEAD_V, (h + 1) * HEAD_V)

    def as_column(r, h):
        d = d_ref[0, 0, r:r + 1, hk(h)]
        return jnp.sum(jnp.where(eye, d, 0.0), axis=1, keepdims=True)

    def scan_tile(q_ref, k_ref, d_row0, mask, order, emit):
        for c in range(BPT):
            for h in range(HEADS):
                q = q_ref[0, rows(c), hk(h)]
                k = k_ref[0, rows(c), hk(h)]
                sc_scr[c, h] = jnp.where(mask, _dot_nt(q, k), 0.0).astype(BF16)
                kv_scr[c, h] = _dot_tn(k, v_ref[0, rows(c), hv(h)])
        for h in range(HEADS):
            state = state_scr[h]
            for c in order:
                s_in = state * as_column(d_row0 + c, h)
                sb_scr[c, h] = s_in.astype(BF16)
                state = (s_in + kv_scr[c, h]) * as_column(d_row0 + BPT + c, h)
            state_scr[h] = state
        for c in range(BPT):
            for h in range(HEADS):
                lhs = jnp.concatenate([q_ref[0, rows(c), hk(h)], sc_scr[c, h]], axis=1)
                rhs = jnp.concatenate([sb_scr[c, h], v_ref[0, rows(c), hv(h)]], axis=0)
                emit(c, h, _dot(lhs, rhs))

    @pl.when(jnp.logical_and(p == 0, i == 0))
    def _():
        state_scr[...] = jnp.zeros_like(state_scr)

    @pl.when(jnp.logical_and(p == 1, i == 0))
    def _():
        state_scr[...] = s0_ref[...]

    @pl.when(p == 0)
    def _():
        base = (nt - 1 - i) * t

        def emit(c, h, o):
            ob_scr[pl.ds(pl.multiple_of(base + c * BLOCK, BLOCK), BLOCK), hv(h)] = o

        scan_tile(qb_ref, kb_ref, 2 * BPT, col > row, list(reversed(range(BPT))), emit)

    @pl.when(p == 1)
    def _():
        base = i * t

        def emit(c, h, o):
            ob = ob_scr[pl.ds(pl.multiple_of(base + c * BLOCK, BLOCK), BLOCK), hv(h)]
            o_scr[rows(c), hv(h)] = o + ob

        scan_tile(qf_ref, kf_ref, 0, col <= row, list(range(BPT)), emit)
        glag = glag_ref[...]
        ys = []
        for h in range(HEADS):
            oh = _rms(o_scr[:, hv(h)], glag)
            ys.append((oh * rs_ref[0, :, hv(h)].astype(F32)).astype(BF16))
        y = jnp.concatenate(ys, axis=-1)
        p_gla = _dot(y, wog_ref[...])
        merged = a_ref[0].astype(F32) + smb_ref[0].astype(F32) * p_gla
        out = _dot(merged.astype(BF16), wo_ref[...])
        out_ref[0] = x_ref[0] + _rms(out, gpost_ref[...])


def kernel(x, meta_tokens, norm_pre, w_in, conv_w, w_gate_fwd, b_gate_fwd, w_gate_bwd,
           b_gate_bwd, gla_norm, w_out_conv, w_out_gla, w_merge_out, norm_post):
    bsz, seq, _ = x.shape
    assert seq % TILE == 0 and norm_pre.shape[0] == 1
    nt = seq // TILE

    w = w_in[0]
    w_main = w[:, :N_MAIN].astype(BF16)
    w_merge = w[:, COL_MERGE:].astype(BF16)
    w_lr = jnp.pad(w[:, COL_LR:COL_MERGE], ((0, 0), (0, LR_PAD - 2 * GATE_RANK))).astype(BF16)
    wg = jnp.zeros((LR_PAD, 2 * DK), F32)
    wg = wg.at[:GATE_RANK, :DK].set(w_gate_fwd[0])
    wg = wg.at[GATE_RANK:2 * GATE_RANK, DK:].set(w_gate_bwd[0]).astype(BF16)
    bg = jnp.concatenate([b_gate_fwd[0], b_gate_bwd[0]])[None, :]
    woc = w_out_conv[0].astype(BF16)
    wog = w_out_gla[0].astype(BF16)
    wo = w_merge_out[0].astype(BF16)
    gpre = norm_pre[0][None, :]
    gpost = norm_post[0][None, :]
    glag = gla_norm[0][None, :]
    convw = conv_w[0]

    def col_spec(width, col):
        return pl.BlockSpec((D_MODEL, width), lambda g, c=col // width: (0, c))

    def full1(shape):
        return pl.BlockSpec(shape, lambda g: (0,) * len(shape))

    smeta, s0 = pl.pallas_call(
        _meta_kernel,
        grid=(1,),
        in_specs=[full1((N_META, D_MODEL)), full1((1, D_MODEL)),
                  col_spec(1024, COL_CC), col_spec(1024, COL_CX),
                  col_spec(DK, COL_K), col_spec(DV, COL_V), full1((D_MODEL, LR_PAD)),
                  full1((LR_PAD, 2 * DK)), full1((1, 2 * DK))],
        out_specs=[full1((N_META, D_MODEL)), full1((HEADS, HEAD_K, HEAD_V))],
        out_shape=[jax.ShapeDtypeStruct((N_META, D_MODEL), F32),
                   jax.ShapeDtypeStruct((HEADS, HEAD_K, HEAD_V), F32)],
        compiler_params=pltpu.CompilerParams(vmem_limit_bytes=VMEM_LIMIT),
        name="meta_prologue",
    )(meta_tokens, gpre, w_main, w_main, w_main, w_main, w_lr, wg, bg)

    hb = TILE // HALO
    n_hb = seq // HALO

    def const2(shape):
        return pl.BlockSpec(shape, lambda b, i: (0,) * len(shape),
                            pipeline_mode=pl.Buffered(1))

    def tok_spec(width):
        return pl.BlockSpec((1, TILE, width), lambda b, i: (b, i, 0))

    d_spec = pl.BlockSpec((1, 1, 4 * BPT, DK), lambda b, i: (b, i, 0, 0))
    tok_shape = lambda width: jax.ShapeDtypeStruct((bsz, seq, width), BF16)
    d_shape = jax.ShapeDtypeStruct((bsz, nt, 4 * BPT, DK), F32)

    qf, kf, qb, kb, v, rs, smb, a, d = pl.pallas_call(
        _proj_kernel,
        grid=(bsz, nt),
        in_specs=[
            tok_spec(D_MODEL),
            pl.BlockSpec((1, HALO, D_MODEL),
                         lambda b, i: (b, jnp.maximum(i * hb - 1, 0), 0)),
            pl.BlockSpec((1, HALO, D_MODEL),
                         lambda b, i: (b, jnp.minimum((i + 1) * hb, n_hb - 1), 0)),
            const2((N_META, D_MODEL)), const2((1, D_MODEL)),
            const2((D_MODEL, N_MAIN)), const2((D_MODEL, 2 * D_MODEL)),
            const2((D_MODEL, LR_PAD)),
            const2((3, D_MODEL)), const2((LR_PAD, 2 * DK)), const2((1, 2 * DK)),
            const2((D_MODEL, D_MODEL)),
        ],
        out_specs=[tok_spec(DK), tok_spec(DK), tok_spec(DK), tok_spec(DK),
                   tok_spec(DV), tok_spec(DV), tok_spec(D_MODEL), tok_spec(D_MODEL),
                   d_spec],
        out_shape=[tok_shape(DK), tok_shape(DK), tok_shape(DK), tok_shape(DK),
                   tok_shape(DV), tok_shape(DV), tok_shape(D_MODEL), tok_shape(D_MODEL),
                   d_shape],
        scratch_shapes=[pltpu.VMEM((TILE + 2 * HALO, D_MODEL), BF16),
                        pltpu.VMEM((TILE + 2 * HALO, CW), F32),
                        pltpu.VMEM((TILE, D_MODEL), BF16)],
        compiler_params=pltpu.CompilerParams(
            dimension_semantics=("parallel", "arbitrary"),
            vmem_limit_bytes=VMEM_LIMIT),
        name="inproj_conv_gates",
    )(x, x, x, smeta, gpre, w_main, w_merge, w_lr, convw, wg, bg, woc)

    def fwd_map(b, p, i):
        return (b, p * i, 0)

    def bwd_map(b, p, i):
        return (b, (1 - p) * (nt - 1 - i), 0)

    def both_map(b, p, i):
        return (b, p * i + (1 - p) * (nt - 1 - i), 0)

    def tok2(width, imap):
        return pl.BlockSpec((1, TILE, width), imap)

    def const3(shape):
        return pl.BlockSpec(shape, lambda b, p, i: (0,) * len(shape))

    d_spec2 = pl.BlockSpec((1, 1, 4 * BPT, DK),
                           lambda b, p, i: (b, p * i + (1 - p) * (nt - 1 - i), 0, 0))

    out = pl.pallas_call(
        _gla_kernel,
        grid=(bsz, 2, nt),
        in_specs=[
            tok2(DK, fwd_map), tok2(DK, fwd_map), tok2(DK, bwd_map), tok2(DK, bwd_map),
            tok2(DV, both_map), tok2(DV, fwd_map), tok2(D_MODEL, fwd_map),
            tok2(D_MODEL, fwd_map), tok2(D_MODEL, fwd_map),
            d_spec2,
            const3((HEADS, HEAD_K, HEAD_V)), const3((1, HEAD_V)),
            const3((DV, D_MODEL)), const3((D_MODEL, D_MODEL)), const3((1, D_MODEL)),
        ],
        out_specs=tok2(D_MODEL, fwd_map),
        out_shape=jax.ShapeDtypeStruct((bsz, seq, D_MODEL), x.dtype),
        scratch_shapes=[pltpu.VMEM((HEADS, HEAD_K, HEAD_V), F32),
                        pltpu.VMEM((seq, DV), F32),
                        pltpu.VMEM((TILE, DV), F32),
                        pltpu.VMEM((BPT, HEADS, BLOCK, BLOCK), BF16),
                        pltpu.VMEM((BPT, HEADS, HEAD_K, HEAD_V), F32),
                        pltpu.VMEM((BPT, HEADS, HEAD_K, HEAD_V), BF16)],
        compiler_params=pltpu.CompilerParams(
            dimension_semantics=("parallel", "arbitrary", "arbitrary"),
            vmem_limit_bytes=VMEM_LIMIT),
        name="gla_merge_out",
    )(qf, kf, qb, kb, v, rs, smb, a, x, d, s0, glag, wog, wo, gpost)
    return out
```

```python
import jax
import jax.numpy as jnp
from jax import lax
from jax.experimental import pallas as pl
from jax.experimental.pallas import tpu as pltpu

D_MODEL = 1024
N_META = 16
HEADS = 4
DK = 512
DV = 1024
HEAD_K = DK // HEADS
HEAD_V = DV // HEADS
GATE_RANK = 16
GATE_NORMALIZER = 16.0
EPS = 1e-6

COL_CB, COL_CC, COL_CX, COL_CZ = 0, 1024, 2048, 3072
COL_Q, COL_K, COL_V, COL_R = 4096, 4608, 5120, 6144
COL_LR = 7168
COL_MERGE = COL_LR + 2 * GATE_RANK
N_IN = COL_MERGE + 2 * D_MODEL
LR_PAD = 128

HALO = 16
TILE = 512
BLOCK = 128
BPT = TILE // BLOCK
CW = 256
VMEM_LIMIT = 56 * 1024 * 1024

F32 = jnp.float32
BF16 = jnp.bfloat16


def _dot(a, b):
    return jnp.dot(a, b, preferred_element_type=F32)


def _dot_nt(a, b):
    return lax.dot_general(a, b, (((1,), (1,)), ((), ())), preferred_element_type=F32)


def _dot_tn(a, b):
    return lax.dot_general(a, b, (((0,), (0,)), ((), ())), preferred_element_type=F32)


def _rms(x, g):
    ms = jnp.mean(x * x, axis=-1, keepdims=True)
    return x * lax.rsqrt(ms + EPS) * g


def _sigmoid(x):
    return 1.0 / (1.0 + jnp.exp(-x))


def _log_sigmoid(x):
    return jnp.minimum(x, 0.0) - jnp.log1p(jnp.exp(-jnp.abs(x)))


def _sum_matrix(n, ref, reverse):
    r = lax.broadcasted_iota(jnp.int32, (n, 2 * n), 0)
    c = lax.broadcasted_iota(jnp.int32, (n, 2 * n), 1)
    c = jnp.where(c >= n, c - n, c)
    if reverse:
        m = jnp.where(c >= r, 1.0, 0.0) - jnp.where(c >= ref, 1.0, 0.0)
    else:
        m = jnp.where(c <= r, 1.0, 0.0) - jnp.where(c < ref, 1.0, 0.0)
    return m.astype(BF16)


def _row_sums(mat2, g):
    hi = g.astype(BF16)
    lo = (g - hi.astype(F32)).astype(BF16)
    return _dot(mat2, jnp.concatenate([hi, lo], axis=0))


def _meta_kernel(meta_ref, gpre_ref, wcc_ref, wcx_ref, wk_ref, wv_ref, wlr_ref,
                 wg_ref, bg_ref, smeta_ref, s0_ref):
    u = _rms(meta_ref[...], gpre_ref[...]).astype(BF16)
    smeta_ref[...] = _dot(u, wcc_ref[...]) * _dot(u, wcx_ref[...])
    k = _dot(u, wk_ref[...])
    v = _dot(u, wv_ref[...]).astype(BF16)
    lr = _dot(u, wlr_ref[...]).astype(BF16)
    z = _dot(lr, wg_ref[:, :DK]) + bg_ref[:, :DK]
    g = _log_sigmoid(z) * (1.0 / GATE_NORMALIZER)
    b = _row_sums(_sum_matrix(N_META, 0, reverse=False), g)
    kdec = (k * jnp.exp(b[N_META - 1:N_META, :] - b)).astype(BF16)
    for h in range(HEADS):
        s0_ref[h] = _dot_tn(kdec[:, h * HEAD_K:(h + 1) * HEAD_K],
                            v[:, h * HEAD_V:(h + 1) * HEAD_V])


def _proj_kernel(x_ref, xp_ref, xn_ref, smeta_ref, gpre_ref, w_ref, wm_ref, wlr_ref,
                 convw_ref, wg_ref, bg_ref, woc_ref,
                 qf_ref, kf_ref, qb_ref, kb_ref, v_ref, rs_ref, smb_ref, a_ref, d_ref,
                 u_scr, s_scr, y_scr, g_scr, q_scr, k_scr):
    i = pl.program_id(1)
    last = pl.num_programs(1) - 1
    t = TILE
    gpre = gpre_ref[...]

    u_scr[0:HALO, :] = _rms(xp_ref[0], gpre).astype(BF16)
    u_scr[HALO:HALO + t, :] = _rms(x_ref[0], gpre).astype(BF16)
    u_scr[HALO + t:, :] = _rms(xn_ref[0], gpre).astype(BF16)

    def w(col, width=CW):
        return w_ref[:, col:col + width]

    u_main = u_scr[HALO:HALO + t, :]
    lr = _dot(u_main, wlr_ref[...]).astype(BF16)
    g_scr[...] = _log_sigmoid(_dot(lr, wg_ref[...]) + bg_ref[...]) * (1.0 / GATE_NORMALIZER)
    q_scr[...] = _dot(u_main, w(COL_Q, DK)) * (HEAD_K ** -0.5)
    k_scr[...] = _dot(u_main, w(COL_K, DK))

    half = BLOCK // 2
    mat_f = _sum_matrix(BLOCK, half, reverse=False)
    mat_b = _sum_matrix(BLOCK, half, reverse=True)

    def decay_block(c):
        rs = slice(c * BLOCK, (c + 1) * BLOCK)
        qc = q_scr[rs, :]
        kc = k_scr[rs, :]
        gf = g_scr[rs, :DK]
        gb = g_scr[rs, DK:]
        bf = _row_sums(mat_f, gf)
        bb = _row_sums(mat_b, gb)
        qf_ref[0, rs, :] = (qc * jnp.exp(bf)).astype(BF16)
        kf_ref[0, rs, :] = (kc * jnp.exp(-bf)).astype(BF16)
        qb_ref[0, rs, :] = (qc * jnp.exp(bb)).astype(BF16)
        kb_ref[0, rs, :] = (kc * jnp.exp(-bb)).astype(BF16)
        d_ref[0, 0, c:c + 1, :] = jnp.exp(gf[0:1, :] - bf[0:1, :])
        d_ref[0, 0, BPT + c:BPT + c + 1, :] = jnp.exp(bf[BLOCK - 1:BLOCK, :])
        d_ref[0, 0, 2 * BPT + c:2 * BPT + c + 1, :] = jnp.exp(
            gb[BLOCK - 1:BLOCK, :] - bb[BLOCK - 1:BLOCK, :])
        d_ref[0, 0, 3 * BPT + c:3 * BPT + c + 1, :] = jnp.exp(bb[0:1, :])

    assert D_MODEL // CW == BPT
    for j in range(D_MODEL // CW):
        c0 = j * CW
        u_ext = u_scr[...]
        s_scr[...] = _dot(u_ext, w(COL_CC + c0)) * _dot(u_ext, w(COL_CX + c0))
        s_scr[HALO - 1:HALO, :] = jnp.where(
            i == 0, smeta_ref[N_META - 1:N_META, c0:c0 + CW], s_scr[HALO - 1:HALO, :])
        s_scr[HALO + t:HALO + t + 1, :] = jnp.where(
            i == last, 0.0, s_scr[HALO + t:HALO + t + 1, :])
        conv = (s_scr[HALO - 1:HALO - 1 + t, :] * convw_ref[0:1, c0:c0 + CW]
                + s_scr[HALO:HALO + t, :] * convw_ref[1:2, c0:c0 + CW]
                + s_scr[HALO + 1:HALO + 1 + t, :] * convw_ref[2:3, c0:c0 + CW])
        cb = _dot(u_main, w(COL_CB + c0))
        cz = _dot(u_main, w(COL_CZ + c0))
        y_scr[:, c0:c0 + CW] = (cb * conv * (cz * _sigmoid(cz))).astype(BF16)
        decay_block(j)

    for j in range(D_MODEL // CW):
        c0 = j * CW
        pc = _dot(y_scr[...], woc_ref[:, c0:c0 + CW])
        ma = _dot(u_main, wm_ref[:, c0:c0 + CW])
        a_ref[0, :, c0:c0 + CW] = (_sigmoid(ma) * pc).astype(BF16)
        mb = _dot(u_main, wm_ref[:, D_MODEL + c0:D_MODEL + c0 + CW])
        smb_ref[0, :, c0:c0 + CW] = _sigmoid(mb).astype(BF16)
        r = _dot(u_main, w(COL_R + c0))
        rs_ref[0, :, c0:c0 + CW] = (r * _sigmoid(r)).astype(BF16)
        v_ref[0, :, c0:c0 + CW] = _dot(u_main, w(COL_V + c0)).astype(BF16)


def _gla_kernel(qf_ref, kf_ref, qb_ref, kb_ref, v_ref, rs_ref, smb_ref, a_ref, x_ref,
                d_ref, s0_ref, glag_ref, wog_ref, wo_ref, gpost_ref,
                out_ref, state_scr, ob_scr, o_scr, sc_scr, kv_scr, sb_scr):
    p = pl.program_id(1)
    i = pl.program_id(2)
    nt = pl.num_programs(2)
    t = TILE

    row = lax.broadcasted_iota(jnp.int32, (BLOCK, BLOCK), 0)
    col = lax.broadcasted_iota(jnp.int32, (BLOCK, BLOCK), 1)
    eye = row == col

    def rows(c):
        return slice(c * BLOCK, (c + 1) * BLOCK)

    def hk(h):
        return slice(h * HEAD_K, (h + 1) * HEAD_K)

    def hv(h):
        return slice(h * HEAD_V, (h + 1) * HEAD_V)

    def as_column(r, h):
        d = d_ref[0, 0, r:r + 1, hk(h)]
        return jnp.sum(jnp.where(eye, d, 0.0), axis=1, keepdims=True)

    def scan_tile(q_ref, k_ref, d_row0, mask, order, emit):
        for c in range(BPT):
            for h in range(HEADS):
                q = q_ref[0, rows(c), hk(h)]
                k = k_ref[0, rows(c), hk(h)]
                sc_scr[c, h] = jnp.where(mask, _dot_nt(q, k), 0.0).astype(BF16)
                kv_scr[c, h] = _dot_tn(k, v_ref[0, rows(c), hv(h)])
        for h in range(HEADS):
            state = state_scr[h]
            for c in order:
                s_in = state * as_column(d_row0 + c, h)
                sb_scr[c, h] = s_in.astype(BF16)
                state = (s_in + kv_scr[c, h]) * as_column(d_row0 + BPT + c, h)
            state_scr[h] = state
        for c in range(BPT):
            for h in range(HEADS):
                lhs = jnp.concatenate([q_ref[0, rows(c), hk(h)], sc_scr[c, h]], axis=1)
                rhs = jnp.concatenate([sb_scr[c, h], v_ref[0, rows(c), hv(h)]], axis=0)
                emit(c, h, _dot(lhs, rhs))

    @pl.when(jnp.logical_and(p == 0, i == 0))
    def _():
        state_scr[...] = jnp.zeros_like(state_scr)

    @pl.when(jnp.logical_and(p == 1, i == 0))
    def _():
        state_scr[...] = s0_ref[...]

    @pl.when(p == 0)
    def _():
        base = (nt - 1 - i) * t

        def emit(c, h, o):
            ob_scr[pl.ds(pl.multiple_of(base + c * BLOCK, BLOCK), BLOCK), hv(h)] = o

        scan_tile(qb_ref, kb_ref, 2 * BPT, col > row, list(reversed(range(BPT))), emit)

    @pl.when(p == 1)
    def _():
        base = i * t

        def emit(c, h, o):
            ob = ob_scr[pl.ds(pl.multiple_of(base + c * BLOCK, BLOCK), BLOCK), hv(h)]
            o_scr[rows(c), hv(h)] = o + ob

        scan_tile(qf_ref, kf_ref, 0, col <= row, list(range(BPT)), emit)
        glag = glag_ref[...]
        ys = []
        for h in range(HEADS):
            oh = _rms(o_scr[:, hv(h)], glag)
            ys.append((oh * rs_ref[0, :, hv(h)].astype(F32)).astype(BF16))
        y = jnp.concatenate(ys, axis=-1)
        p_gla = _dot(y, wog_ref[...])
        merged = a_ref[0].astype(F32) + smb_ref[0].astype(F32) * p_gla
        out = _dot(merged.astype(BF16), wo_ref[...])
        out_ref[0] = x_ref[0] + _rms(out, gpost_ref[...])


def kernel(x, meta_tokens, norm_pre, w_in, conv_w, w_gate_fwd, b_gate_fwd, w_gate_bwd,
           b_gate_bwd, gla_norm, w_out_conv, w_out_gla, w_merge_out, norm_post):
    bsz, seq, _ = x.shape
    assert seq % TILE == 0 and norm_pre.shape[0] == 1
    nt = seq // TILE

    w_main = w_in[0].astype(BF16)
    w_merge = w_main[:, COL_MERGE:]
    w_lr = jnp.pad(w_main[:, COL_LR:COL_MERGE], ((0, 0), (0, LR_PAD - 2 * GATE_RANK)))
    wg = jnp.zeros((LR_PAD, 2 * DK), F32)
    wg = wg.at[:GATE_RANK, :DK].set(w_gate_fwd[0])
    wg = wg.at[GATE_RANK:2 * GATE_RANK, DK:].set(w_gate_bwd[0]).astype(BF16)
    bg = jnp.concatenate([b_gate_fwd[0], b_gate_bwd[0]])[None, :]
    woc = w_out_conv[0].astype(BF16)
    wog = w_out_gla[0].astype(BF16)
    wo = w_merge_out[0].astype(BF16)
    gpre = norm_pre[0][None, :]
    gpost = norm_post[0][None, :]
    glag = gla_norm[0][None, :]
    convw = conv_w[0]

    def col_spec(width, col):
        return pl.BlockSpec((D_MODEL, width), lambda g, c=col // width: (0, c))

    def full1(shape):
        return pl.BlockSpec(shape, lambda g: (0,) * len(shape))

    smeta, s0 = pl.pallas_call(
        _meta_kernel,
        grid=(1,),
        in_specs=[full1((N_META, D_MODEL)), full1((1, D_MODEL)),
                  col_spec(1024, COL_CC), col_spec(1024, COL_CX),
                  col_spec(DK, COL_K), col_spec(DV, COL_V), full1((D_MODEL, LR_PAD)),
                  full1((LR_PAD, 2 * DK)), full1((1, 2 * DK))],
        out_specs=[full1((N_META, D_MODEL)), full1((HEADS, HEAD_K, HEAD_V))],
        out_shape=[jax.ShapeDtypeStruct((N_META, D_MODEL), F32),
                   jax.ShapeDtypeStruct((HEADS, HEAD_K, HEAD_V), F32)],
        compiler_params=pltpu.CompilerParams(vmem_limit_bytes=VMEM_LIMIT),
        name="meta_prologue",
    )(meta_tokens, gpre, w_main, w_main, w_main, w_main, w_lr, wg, bg)

    hb = TILE // HALO
    n_hb = seq // HALO

    def const2(shape):
        return pl.BlockSpec(shape, lambda b, i: (0,) * len(shape),
                            pipeline_mode=pl.Buffered(1))

    def tok_spec(width):
        return pl.BlockSpec((1, TILE, width), lambda b, i: (b, i, 0))

    d_spec = pl.BlockSpec((1, 1, 4 * BPT, DK), lambda b, i: (b, i, 0, 0))
    tok_shape = lambda width: jax.ShapeDtypeStruct((bsz, seq, width), BF16)
    d_shape = jax.ShapeDtypeStruct((bsz, nt, 4 * BPT, DK), F32)

    qf, kf, qb, kb, v, rs, smb, a, d = pl.pallas_call(
        _proj_kernel,
        grid=(bsz, nt),
        in_specs=[
            tok_spec(D_MODEL),
            pl.BlockSpec((1, HALO, D_MODEL),
                         lambda b, i: (b, jnp.maximum(i * hb - 1, 0), 0)),
            pl.BlockSpec((1, HALO, D_MODEL),
                         lambda b, i: (b, jnp.minimum((i + 1) * hb, n_hb - 1), 0)),
            const2((N_META, D_MODEL)), const2((1, D_MODEL)),
            const2((D_MODEL, N_IN)), const2((D_MODEL, 2 * D_MODEL)),
            const2((D_MODEL, LR_PAD)),
            const2((3, D_MODEL)), const2((LR_PAD, 2 * DK)), const2((1, 2 * DK)),
            const2((D_MODEL, D_MODEL)),
        ],
        out_specs=[tok_spec(DK), tok_spec(DK), tok_spec(DK), tok_spec(DK),
                   tok_spec(DV), tok_spec(DV), tok_spec(D_MODEL), tok_spec(D_MODEL),
                   d_spec],
        out_shape=[tok_shape(DK), tok_shape(DK), tok_shape(DK), tok_shape(DK),
                   tok_shape(DV), tok_shape(DV), tok_shape(D_MODEL), tok_shape(D_MODEL),
                   d_shape],
        scratch_shapes=[pltpu.VMEM((TILE + 2 * HALO, D_MODEL), BF16),
                        pltpu.VMEM((TILE + 2 * HALO, CW), F32),
                        pltpu.VMEM((TILE, D_MODEL), BF16),
                        pltpu.VMEM((TILE, 2 * DK), F32),
                        pltpu.VMEM((TILE, DK), F32),
                        pltpu.VMEM((TILE, DK), F32)],
        compiler_params=pltpu.CompilerParams(
            dimension_semantics=("parallel", "arbitrary"),
            vmem_limit_bytes=VMEM_LIMIT),
        name="inproj_conv_gates",
    )(x, x, x, smeta, gpre, w_main, w_merge, w_lr, convw, wg, bg, woc)

    def fwd_map(b, p, i):
        return (b, p * i, 0)

    def bwd_map(b, p, i):
        return (b, (1 - p) * (nt - 1 - i), 0)

    def both_map(b, p, i):
        return (b, p * i + (1 - p) * (nt - 1 - i), 0)

    def tok2(width, imap):
        return pl.BlockSpec((1, TILE, width), imap)

    def const3(shape):
        return pl.BlockSpec(shape, lambda b, p, i: (0,) * len(shape))

    d_spec2 = pl.BlockSpec((1, 1, 4 * BPT, DK),
                           lambda b, p, i: (b, p * i + (1 - p) * (nt - 1 - i), 0, 0))

    out = pl.pallas_call(
        _gla_kernel,
        grid=(bsz, 2, nt),
        in_specs=[
            tok2(DK, fwd_map), tok2(DK, fwd_map), tok2(DK, bwd_map), tok2(DK, bwd_map),
            tok2(DV, both_map), tok2(DV, fwd_map), tok2(D_MODEL, fwd_map),
            tok2(D_MODEL, fwd_map), tok2(D_MODEL, fwd_map),
            d_spec2,
            const3((HEADS, HEAD_K, HEAD_V)), const3((1, HEAD_V)),
            const3((DV, D_MODEL)), const3((D_MODEL, D_MODEL)), const3((1, D_MODEL)),
        ],
        out_specs=tok2(D_MODEL, fwd_map),
        out_shape=jax.ShapeDtypeStruct((bsz, seq, D_MODEL), x.dtype),
        scratch_shapes=[pltpu.VMEM((HEADS, HEAD_K, HEAD_V), F32),
                        pltpu.VMEM((seq, DV), F32),
                        pltpu.VMEM((TILE, DV), F32),
                        pltpu.VMEM((BPT, HEADS, BLOCK, BLOCK), BF16),
                        pltpu.VMEM((BPT, HEADS, HEAD_K, HEAD_V), F32),
                        pltpu.VMEM((BPT, HEADS, HEAD_K, HEAD_V), BF16)],
        compiler_params=pltpu.CompilerParams(
            dimension_semantics=("parallel", "arbitrary", "arbitrary"),
            vmem_limit_bytes=VMEM_LIMIT),
        name="gla_merge_out",
    )(qf, kf, qb, kb, v, rs, smb, a, x, d, s0, glag, wog, wo, gpost)
    return out
```

```python
import jax
import jax.numpy as jnp
from jax import lax
from jax.experimental import pallas as pl
from jax.experimental.pallas import tpu as pltpu

D_MODEL = 1024
N_META = 16
HEADS = 4
DK = 512
DV = 1024
HEAD_K = DK // HEADS
HEAD_V = DV // HEADS
GATE_RANK = 16
GATE_NORMALIZER = 16.0
EPS = 1e-6

COL_CB, COL_CC, COL_CX, COL_CZ = 0, 1024, 2048, 3072
COL_Q, COL_K, COL_V, COL_R = 4096, 4608, 5120, 6144
COL_LR = 7168
COL_MERGE = COL_LR + 2 * GATE_RANK
N_IN = COL_MERGE + 2 * D_MODEL
LR_PAD = 128

HALO = 16
TILE = 512
BLOCK = 128
BPT = TILE // BLOCK
D_ROWS = 128
CW = 256
VMEM_LIMIT = 56 * 1024 * 1024

F32 = jnp.float32
BF16 = jnp.bfloat16


def _dot(a, b):
    return jnp.dot(a, b, preferred_element_type=F32)


def _dot_nt(a, b):
    return lax.dot_general(a, b, (((1,), (1,)), ((), ())), preferred_element_type=F32)


def _dot_tn(a, b):
    return lax.dot_general(a, b, (((0,), (0,)), ((), ())), preferred_element_type=F32)


def _rms(x, g):
    ms = jnp.mean(x * x, axis=-1, keepdims=True)
    return x * lax.rsqrt(ms + EPS) * g


def _sigmoid(x):
    return 1.0 / (1.0 + jnp.exp(-x))


def _log_sigmoid(x):
    return jnp.minimum(x, 0.0) - jnp.log1p(jnp.exp(-jnp.abs(x)))


def _sum_matrix(n, ref, reverse):
    r = lax.broadcasted_iota(jnp.int32, (n, 2 * n), 0)
    c = lax.broadcasted_iota(jnp.int32, (n, 2 * n), 1)
    c = jnp.where(c >= n, c - n, c)
    if reverse:
        m = jnp.where(c >= r, 1.0, 0.0) - jnp.where(c >= ref, 1.0, 0.0)
    else:
        m = jnp.where(c <= r, 1.0, 0.0) - jnp.where(c < ref, 1.0, 0.0)
    return m.astype(BF16)


def _row_sums(mat2, g):
    hi = g.astype(BF16)
    lo = (g - hi.astype(F32)).astype(BF16)
    return _dot(mat2, jnp.concatenate([hi, lo], axis=0))


def _meta_kernel(meta_ref, gpre_ref, wcc_ref, wcx_ref, wk_ref, wv_ref, wlr_ref,
                 wg_ref, bg_ref, smeta_ref, s0_ref):
    u = _rms(meta_ref[...], gpre_ref[...]).astype(BF16)
    smeta_ref[...] = _dot(u, wcc_ref[...]) * _dot(u, wcx_ref[...])
    k = _dot(u, wk_ref[...])
    v = _dot(u, wv_ref[...]).astype(BF16)
    lr = _dot(u, wlr_ref[...]).astype(BF16)
    z = _dot(lr, wg_ref[:, :DK]) + bg_ref[:, :DK]
    g = _log_sigmoid(z) * (1.0 / GATE_NORMALIZER)
    b = _row_sums(_sum_matrix(N_META, 0, reverse=False), g)
    kdec = (k * jnp.exp(b[N_META - 1:N_META, :] - b)).astype(BF16)
    for h in range(HEADS):
        s0_ref[h] = _dot_tn(kdec[:, h * HEAD_K:(h + 1) * HEAD_K],
                            v[:, h * HEAD_V:(h + 1) * HEAD_V])


def _proj_kernel(x_ref, xp_ref, xn_ref, smeta_ref, gpre_ref, w_ref, wm_ref, wlr_ref,
                 convw_ref, wg_ref, bg_ref, woc_ref,
                 qf_ref, kft_ref, qb_ref, kbt_ref, v_ref, rs_ref, smb_ref, a_ref, dt_ref,
                 u_scr, s_scr, y_scr, g_scr, q_scr, k_scr, drow_scr):
    i = pl.program_id(1)
    last = pl.num_programs(1) - 1
    t = TILE
    gpre = gpre_ref[...]

    u_scr[0:HALO, :] = _rms(xp_ref[0], gpre).astype(BF16)
    u_scr[HALO:HALO + t, :] = _rms(x_ref[0], gpre).astype(BF16)
    u_scr[HALO + t:, :] = _rms(xn_ref[0], gpre).astype(BF16)

    def w(col, width=CW):
        return w_ref[:, col:col + width]

    u_main = u_scr[HALO:HALO + t, :]
    lr = _dot(u_main, wlr_ref[...]).astype(BF16)
    g_scr[...] = _log_sigmoid(_dot(lr, wg_ref[...]) + bg_ref[...]) * (1.0 / GATE_NORMALIZER)
    q_scr[...] = _dot(u_main, w(COL_Q, DK)) * (HEAD_K ** -0.5)
    k_scr[...] = _dot(u_main, w(COL_K, DK))

    half = BLOCK // 2
    mat_f = _sum_matrix(BLOCK, half, reverse=False)
    mat_b = _sum_matrix(BLOCK, half, reverse=True)

    def decay_block(c):
        rs = slice(c * BLOCK, (c + 1) * BLOCK)
        qc = q_scr[rs, :]
        kc = k_scr[rs, :]
        gf = g_scr[rs, :DK]
        gb = g_scr[rs, DK:]
        bf = _row_sums(mat_f, gf)
        bb = _row_sums(mat_b, gb)
        qf_ref[0, rs, :] = (qc * jnp.exp(bf)).astype(BF16)
        qb_ref[0, rs, :] = (qc * jnp.exp(bb)).astype(BF16)
        kft_ref[0, :, rs] = (kc * jnp.exp(-bf)).T.astype(BF16)
        kbt_ref[0, :, rs] = (kc * jnp.exp(-bb)).T.astype(BF16)
        drow_scr[c:c + 1, :] = jnp.exp(gf[0:1, :] - bf[0:1, :])
        drow_scr[BPT + c:BPT + c + 1, :] = jnp.exp(bf[BLOCK - 1:BLOCK, :])
        drow_scr[2 * BPT + c:2 * BPT + c + 1, :] = jnp.exp(
            gb[BLOCK - 1:BLOCK, :] - bb[BLOCK - 1:BLOCK, :])
        drow_scr[3 * BPT + c:3 * BPT + c + 1, :] = jnp.exp(bb[0:1, :])

    drow_scr[4 * BPT:, :] = jnp.zeros((D_ROWS - 4 * BPT, DK), F32)

    assert D_MODEL // CW == BPT
    for j in range(D_MODEL // CW):
        c0 = j * CW
        u_ext = u_scr[...]
        s_scr[...] = _dot(u_ext, w(COL_CC + c0)) * _dot(u_ext, w(COL_CX + c0))
        s_scr[HALO - 1:HALO, :] = jnp.where(
            i == 0, smeta_ref[N_META - 1:N_META, c0:c0 + CW], s_scr[HALO - 1:HALO, :])
        s_scr[HALO + t:HALO + t + 1, :] = jnp.where(
            i == last, 0.0, s_scr[HALO + t:HALO + t + 1, :])
        conv = (s_scr[HALO - 1:HALO - 1 + t, :] * convw_ref[0:1, c0:c0 + CW]
                + s_scr[HALO:HALO + t, :] * convw_ref[1:2, c0:c0 + CW]
                + s_scr[HALO + 1:HALO + 1 + t, :] * convw_ref[2:3, c0:c0 + CW])
        cb = _dot(u_main, w(COL_CB + c0))
        cz = _dot(u_main, w(COL_CZ + c0))
        y_scr[:, c0:c0 + CW] = (cb * conv * (cz * _sigmoid(cz))).astype(BF16)
        decay_block(j)

    dt_ref[0, 0] = drow_scr[...].T

    for j in range(D_MODEL // CW):
        c0 = j * CW
        pc = _dot(y_scr[...], woc_ref[:, c0:c0 + CW])
        ma = _dot(u_main, wm_ref[:, c0:c0 + CW])
        a_ref[0, :, c0:c0 + CW] = (_sigmoid(ma) * pc).astype(BF16)
        mb = _dot(u_main, wm_ref[:, D_MODEL + c0:D_MODEL + c0 + CW])
        smb_ref[0, :, c0:c0 + CW] = _sigmoid(mb).astype(BF16)
        r = _dot(u_main, w(COL_R + c0))
        rs_ref[0, :, c0:c0 + CW] = (r * _sigmoid(r)).astype(BF16)
        v_ref[0, :, c0:c0 + CW] = _dot(u_main, w(COL_V + c0)).astype(BF16)


def _gla_kernel(qf_ref, kft_ref, qb_ref, kbt_ref, v_ref, rs_ref, smb_ref, a_ref, x_ref,
                dt_ref, s0_ref, glag_ref, wog_ref, wo_ref, gpost_ref,
                out_ref, state_scr, ob_scr, o_scr, sc_scr, kv_scr, sb_scr):
    p = pl.program_id(1)
    i = pl.program_id(2)
    nt = pl.num_programs(2)
    t = TILE

    row = lax.broadcasted_iota(jnp.int32, (BLOCK, BLOCK), 0)
    col = lax.broadcasted_iota(jnp.int32, (BLOCK, BLOCK), 1)

    def rows(c):
        return slice(c * BLOCK, (c + 1) * BLOCK)

    def hk(h):
        return slice(h * HEAD_K, (h + 1) * HEAD_K)

    def hv(h):
        return slice(h * HEAD_V, (h + 1) * HEAD_V)

    def decay(r, h):
        return dt_ref[0, 0, hk(h), r:r + 1]

    def scan_tile(q_ref, kt_ref, d_row0, mask, order, emit):
        def products(h):
            for c in range(BPT):
                kt = kt_ref[0, hk(h), rows(c)]
                sc = _dot(q_ref[0, rows(c), hk(h)], kt)
                sc_scr[c, h] = jnp.where(mask, sc, 0.0).astype(BF16)
                kv_scr[c, h] = _dot(kt, v_ref[0, rows(c), hv(h)])

        def carry(h):
            state = state_scr[h]
            for c in order:
                s_in = state * decay(d_row0 + c, h)
                sb_scr[c, h] = s_in.astype(BF16)
                state = (s_in + kv_scr[c, h]) * decay(d_row0 + BPT + c, h)
            state_scr[h] = state

        def outputs(h):
            for c in range(BPT):
                lhs = jnp.concatenate([q_ref[0, rows(c), hk(h)], sc_scr[c, h]], axis=1)
                rhs = jnp.concatenate([sb_scr[c, h], v_ref[0, rows(c), hv(h)]], axis=0)
                emit(c, h, _dot(lhs, rhs))

        products(0)
        for h in range(HEADS):
            if h + 1 < HEADS:
                products(h + 1)
            carry(h)
            outputs(h)

    @pl.when(jnp.logical_and(p == 0, i == 0))
    def _():
        state_scr[...] = jnp.zeros_like(state_scr)

    @pl.when(jnp.logical_and(p == 1, i == 0))
    def _():
        state_scr[...] = s0_ref[...]

    @pl.when(p == 0)
    def _():
        base = (nt - 1 - i) * t

        def emit(c, h, o):
            ob_scr[pl.ds(pl.multiple_of(base + c * BLOCK, BLOCK), BLOCK), hv(h)] = o

        scan_tile(qb_ref, kbt_ref, 2 * BPT, col > row, list(reversed(range(BPT))), emit)

    @pl.when(p == 1)
    def _():
        base = i * t

        def emit(c, h, o):
            ob = ob_scr[pl.ds(pl.multiple_of(base + c * BLOCK, BLOCK), BLOCK), hv(h)]
            o_scr[rows(c), hv(h)] = o + ob

        scan_tile(qf_ref, kft_ref, 0, col <= row, list(range(BPT)), emit)
        glag = glag_ref[...]
        for r in range(2):
            rsl = slice(r * (t // 2), (r + 1) * (t // 2))
            ys = []
            for h in range(HEADS):
                oh = _rms(o_scr[rsl, hv(h)], glag)
                ys.append((oh * rs_ref[0, rsl, hv(h)].astype(F32)).astype(BF16))
            p_gla = _dot(jnp.concatenate(ys, axis=-1), wog_ref[...])
            merged = a_ref[0, rsl, :].astype(F32) + smb_ref[0, rsl, :].astype(F32) * p_gla
            out = _dot(merged.astype(BF16), wo_ref[...])
            out_ref[0, rsl, :] = x_ref[0, rsl, :] + _rms(out, gpost_ref[...])


def kernel(x, meta_tokens, norm_pre, w_in, conv_w, w_gate_fwd, b_gate_fwd, w_gate_bwd,
           b_gate_bwd, gla_norm, w_out_conv, w_out_gla, w_merge_out, norm_post):
    bsz, seq, _ = x.shape
    assert seq % TILE == 0 and norm_pre.shape[0] == 1
    nt = seq // TILE

    w_main = w_in[0].astype(BF16)
    w_merge = w_main[:, COL_MERGE:]
    w_lr = jnp.pad(w_main[:, COL_LR:COL_MERGE], ((0, 0), (0, LR_PAD - 2 * GATE_RANK)))
    wg = jnp.zeros((LR_PAD, 2 * DK), F32)
    wg = wg.at[:GATE_RANK, :DK].set(w_gate_fwd[0])
    wg = wg.at[GATE_RANK:2 * GATE_RANK, DK:].set(w_gate_bwd[0]).astype(BF16)
    bg = jnp.concatenate([b_gate_fwd[0], b_gate_bwd[0]])[None, :]
    woc = w_out_conv[0].astype(BF16)
    wog = w_out_gla[0].astype(BF16)
    wo = w_merge_out[0].astype(BF16)
    gpre = norm_pre[0][None, :]
    gpost = norm_post[0][None, :]
    glag = gla_norm[0][None, :]
    convw = conv_w[0]

    def col_spec(width, col):
        return pl.BlockSpec((D_MODEL, width), lambda g, c=col // width: (0, c))

    def full1(shape):
        return pl.BlockSpec(shape, lambda g: (0,) * len(shape))

    smeta, s0 = pl.pallas_call(
        _meta_kernel,
        grid=(1,),
        in_specs=[full1((N_META, D_MODEL)), full1((1, D_MODEL)),
                  col_spec(1024, COL_CC), col_spec(1024, COL_CX),
                  col_spec(DK, COL_K), col_spec(DV, COL_V), full1((D_MODEL, LR_PAD)),
                  full1((LR_PAD, 2 * DK)), full1((1, 2 * DK))],
        out_specs=[full1((N_META, D_MODEL)), full1((HEADS, HEAD_K, HEAD_V))],
        out_shape=[jax.ShapeDtypeStruct((N_META, D_MODEL), F32),
                   jax.ShapeDtypeStruct((HEADS, HEAD_K, HEAD_V), F32)],
        compiler_params=pltpu.CompilerParams(vmem_limit_bytes=VMEM_LIMIT),
        name="meta_prologue",
    )(meta_tokens, gpre, w_main, w_main, w_main, w_main, w_lr, wg, bg)

    hb = TILE // HALO
    n_hb = seq // HALO

    def const2(shape):
        return pl.BlockSpec(shape, lambda b, i: (0,) * len(shape),
                            pipeline_mode=pl.Buffered(1))

    def tok_spec(width):
        return pl.BlockSpec((1, TILE, width), lambda b, i: (b, i, 0))

    kt_spec = pl.BlockSpec((1, DK, TILE), lambda b, i: (b, 0, i))
    d_spec = pl.BlockSpec((1, 1, DK, D_ROWS), lambda b, i: (b, i, 0, 0))
    tok_shape = lambda width: jax.ShapeDtypeStruct((bsz, seq, width), BF16)
    kt_shape = jax.ShapeDtypeStruct((bsz, DK, seq), BF16)
    d_shape = jax.ShapeDtypeStruct((bsz, nt, DK, D_ROWS), F32)

    qf, kft, qb, kbt, v, rs, smb, a, dt = pl.pallas_call(
        _proj_kernel,
        grid=(bsz, nt),
        in_specs=[
            tok_spec(D_MODEL),
            pl.BlockSpec((1, HALO, D_MODEL),
                         lambda b, i: (b, jnp.maximum(i * hb - 1, 0), 0)),
            pl.BlockSpec((1, HALO, D_MODEL),
                         lambda b, i: (b, jnp.minimum((i + 1) * hb, n_hb - 1), 0)),
            const2((N_META, D_MODEL)), const2((1, D_MODEL)),
            const2((D_MODEL, N_IN)), const2((D_MODEL, 2 * D_MODEL)),
            const2((D_MODEL, LR_PAD)),
            const2((3, D_MODEL)), const2((LR_PAD, 2 * DK)), const2((1, 2 * DK)),
            const2((D_MODEL, D_MODEL)),
        ],
        out_specs=[tok_spec(DK), kt_spec, tok_spec(DK), kt_spec,
                   tok_spec(DV), tok_spec(DV), tok_spec(D_MODEL), tok_spec(D_MODEL),
                   d_spec],
        out_shape=[tok_shape(DK), kt_shape, tok_shape(DK), kt_shape,
                   tok_shape(DV), tok_shape(DV), tok_shape(D_MODEL), tok_shape(D_MODEL),
                   d_shape],
        scratch_shapes=[pltpu.VMEM((TILE + 2 * HALO, D_MODEL), BF16),
                        pltpu.VMEM((TILE + 2 * HALO, CW), F32),
                        pltpu.VMEM((TILE, D_MODEL), BF16),
                        pltpu.VMEM((TILE, 2 * DK), F32),
                        pltpu.VMEM((TILE, DK), F32),
                        pltpu.VMEM((TILE, DK), F32),
                        pltpu.VMEM((D_ROWS, DK), F32)],
        compiler_params=pltpu.CompilerParams(
            dimension_semantics=("parallel", "arbitrary"),
            vmem_limit_bytes=VMEM_LIMIT),
        name="inproj_conv_gates",
    )(x, x, x, smeta, gpre, w_main, w_merge, w_lr, convw, wg, bg, woc)

    def fwd_map(b, p, i):
        return (b, p * i, 0)

    def bwd_map(b, p, i):
        return (b, (1 - p) * (nt - 1 - i), 0)

    def both_map(b, p, i):
        return (b, p * i + (1 - p) * (nt - 1 - i), 0)

    def tok2(width, imap):
        return pl.BlockSpec((1, TILE, width), imap)

    def const3(shape):
        return pl.BlockSpec(shape, lambda b, p, i: (0,) * len(shape))

    def kt2(imap):
        return pl.BlockSpec((1, DK, TILE), lambda b, p, i: (b, 0, imap(b, p, i)[1]))

    d_spec2 = pl.BlockSpec((1, 1, DK, D_ROWS),
                           lambda b, p, i: (b, p * i + (1 - p) * (nt - 1 - i), 0, 0))

    out = pl.pallas_call(
        _gla_kernel,
        grid=(bsz, 2, nt),
        in_specs=[
            tok2(DK, fwd_map), kt2(fwd_map), tok2(DK, bwd_map), kt2(bwd_map),
            tok2(DV, both_map), tok2(DV, fwd_map), tok2(D_MODEL, fwd_map),
            tok2(D_MODEL, fwd_map), tok2(D_MODEL, fwd_map),
            d_spec2,
            const3((HEADS, HEAD_K, HEAD_V)), const3((1, HEAD_V)),
            const3((DV, D_MODEL)), const3((D_MODEL, D_MODEL)), const3((1, D_MODEL)),
        ],
        out_specs=tok2(D_MODEL, fwd_map),
        out_shape=jax.ShapeDtypeStruct((bsz, seq, D_MODEL), x.dtype),
        scratch_shapes=[pltpu.VMEM((HEADS, HEAD_K, HEAD_V), F32),
                        pltpu.VMEM((seq, DV), F32),
                        pltpu.VMEM((TILE, DV), F32),
                        pltpu.VMEM((BPT, HEADS, BLOCK, BLOCK), BF16),
                        pltpu.VMEM((BPT, HEADS, HEAD_K, HEAD_V), F32),
                        pltpu.VMEM((BPT, HEADS, HEAD_K, HEAD_V), BF16)],
        compiler_params=pltpu.CompilerParams(
            dimension_semantics=("parallel", "arbitrary", "arbitrary"),
            vmem_limit_bytes=VMEM_LIMIT),
        name="gla_merge_out",
    )(qf, kft, qb, kbt, v, rs, smb, a, x, dt, s0, glag, wog, wo, gpost)
    return out
```

```python
import jax
import jax.numpy as jnp
from jax import lax
from jax.experimental import pallas as pl
from jax.experimental.pallas import tpu as pltpu

D_MODEL = 1024
N_META = 16
HEADS = 4
DK = 512
DV = 1024
HEAD_K = DK // HEADS
HEAD_V = DV // HEADS
GATE_RANK = 16
GATE_NORMALIZER = 16.0
EPS = 1e-6

COL_CB, COL_CC, COL_CX, COL_CZ = 0, 1024, 2048, 3072
COL_Q, COL_K, COL_V, COL_R = 4096, 4608, 5120, 6144
COL_LR = 7168
COL_MERGE = COL_LR + 2 * GATE_RANK
N_IN = COL_MERGE + 2 * D_MODEL
LR_PAD = 128

HALO = 16
TILE = 512
BLOCK = 128
BPT = TILE // BLOCK
D_ROWS = 128
D_COLS = 4 * BPT
CW = 256
VMEM_LIMIT = 56 * 1024 * 1024

F32 = jnp.float32
BF16 = jnp.bfloat16


def _dot(a, b):
    return jnp.dot(a, b, preferred_element_type=F32)


def _dot_nt(a, b):
    return lax.dot_general(a, b, (((1,), (1,)), ((), ())), preferred_element_type=F32)


def _dot_tn(a, b):
    return lax.dot_general(a, b, (((0,), (0,)), ((), ())), preferred_element_type=F32)


def _rms(x, g):
    ms = jnp.mean(x * x, axis=-1, keepdims=True)
    return x * lax.rsqrt(ms + EPS) * g


def _sigmoid(x):
    return 1.0 / (1.0 + jnp.exp(-x))


def _log_sigmoid(x):
    return jnp.minimum(x, 0.0) - jnp.log1p(jnp.exp(-jnp.abs(x)))


def _sum_matrix(n, ref, reverse):
    r = lax.broadcasted_iota(jnp.int32, (n, 2 * n), 0)
    c = lax.broadcasted_iota(jnp.int32, (n, 2 * n), 1)
    c = jnp.where(c >= n, c - n, c)
    if reverse:
        m = jnp.where(c >= r, 1.0, 0.0) - jnp.where(c >= ref, 1.0, 0.0)
    else:
        m = jnp.where(c <= r, 1.0, 0.0) - jnp.where(c < ref, 1.0, 0.0)
    return m.astype(BF16)


def _row_sums(mat2, g):
    hi = g.astype(BF16)
    lo = (g - hi.astype(F32)).astype(BF16)
    return _dot(mat2, jnp.concatenate([hi, lo], axis=0))


def _meta_kernel(meta_ref, gpre_ref, wcc_ref, wcx_ref, wk_ref, wv_ref, wlr_ref,
                 wg_ref, bg_ref, smeta_ref, s0_ref):
    u = _rms(meta_ref[...], gpre_ref[...]).astype(BF16)
    smeta_ref[...] = _dot(u, wcc_ref[...]) * _dot(u, wcx_ref[...])
    k = _dot(u, wk_ref[...])
    v = _dot(u, wv_ref[...]).astype(BF16)
    lr = _dot(u, wlr_ref[...]).astype(BF16)
    z = _dot(lr, wg_ref[:, :DK]) + bg_ref[:, :DK]
    g = _log_sigmoid(z) * (1.0 / GATE_NORMALIZER)
    b = _row_sums(_sum_matrix(N_META, 0, reverse=False), g)
    kdec = (k * jnp.exp(b[N_META - 1:N_META, :] - b)).astype(BF16)
    for h in range(HEADS):
        s0_ref[h] = _dot_tn(kdec[:, h * HEAD_K:(h + 1) * HEAD_K],
                            v[:, h * HEAD_V:(h + 1) * HEAD_V])


def _proj_kernel(x_ref, xp_ref, xn_ref, smeta_ref, gpre_ref, w_ref, wm_ref, wlr_ref,
                 convw_ref, wg_ref, bg_ref, woc_ref,
                 qf_ref, kft_ref, qb_ref, kbt_ref, v_ref, rs_ref, smb_ref, a_ref, dt_ref,
                 u_scr, s_scr, y_scr, g_scr, q_scr, k_scr, drow_scr):
    i = pl.program_id(1)
    last = pl.num_programs(1) - 1
    t = TILE
    gpre = gpre_ref[...]

    u_scr[0:HALO, :] = _rms(xp_ref[0], gpre).astype(BF16)
    u_scr[HALO:HALO + t, :] = _rms(x_ref[0], gpre).astype(BF16)
    u_scr[HALO + t:, :] = _rms(xn_ref[0], gpre).astype(BF16)

    def w(col, width=CW):
        return w_ref[:, col:col + width]

    u_main = u_scr[HALO:HALO + t, :]
    lr = _dot(u_main, wlr_ref[...]).astype(BF16)
    g_scr[...] = _log_sigmoid(_dot(lr, wg_ref[...]) + bg_ref[...]) * (1.0 / GATE_NORMALIZER)
    q_scr[...] = _dot(u_main, w(COL_Q, DK)) * (HEAD_K ** -0.5)
    k_scr[...] = _dot(u_main, w(COL_K, DK))

    half = BLOCK // 2
    mat_f = _sum_matrix(BLOCK, half, reverse=False)
    mat_b = _sum_matrix(BLOCK, half, reverse=True)

    def decay_block(c):
        rs = slice(c * BLOCK, (c + 1) * BLOCK)
        qc = q_scr[rs, :]
        kc = k_scr[rs, :]
        gf = g_scr[rs, :DK]
        gb = g_scr[rs, DK:]
        bf = _row_sums(mat_f, gf)
        bb = _row_sums(mat_b, gb)
        qf_ref[0, rs, :] = (qc * jnp.exp(bf)).astype(BF16)
        qb_ref[0, rs, :] = (qc * jnp.exp(bb)).astype(BF16)
        kft_ref[0, :, rs] = (kc * jnp.exp(-bf)).T.astype(BF16)
        kbt_ref[0, :, rs] = (kc * jnp.exp(-bb)).T.astype(BF16)
        drow_scr[c:c + 1, :] = jnp.exp(gf[0:1, :] - bf[0:1, :])
        drow_scr[BPT + c:BPT + c + 1, :] = jnp.exp(bf[BLOCK - 1:BLOCK, :])
        drow_scr[2 * BPT + c:2 * BPT + c + 1, :] = jnp.exp(
            gb[BLOCK - 1:BLOCK, :] - bb[BLOCK - 1:BLOCK, :])
        drow_scr[3 * BPT + c:3 * BPT + c + 1, :] = jnp.exp(bb[0:1, :])

    drow_scr[4 * BPT:, :] = jnp.zeros((D_ROWS - 4 * BPT, DK), F32)

    assert D_MODEL // CW == BPT
    for j in range(D_MODEL // CW):
        c0 = j * CW
        u_ext = u_scr[...]
        s_scr[...] = _dot(u_ext, w(COL_CC + c0)) * _dot(u_ext, w(COL_CX + c0))
        s_scr[HALO - 1:HALO, :] = jnp.where(
            i == 0, smeta_ref[N_META - 1:N_META, c0:c0 + CW], s_scr[HALO - 1:HALO, :])
        s_scr[HALO + t:HALO + t + 1, :] = jnp.where(
            i == last, 0.0, s_scr[HALO + t:HALO + t + 1, :])
        conv = (s_scr[HALO - 1:HALO - 1 + t, :] * convw_ref[0:1, c0:c0 + CW]
                + s_scr[HALO:HALO + t, :] * convw_ref[1:2, c0:c0 + CW]
                + s_scr[HALO + 1:HALO + 1 + t, :] * convw_ref[2:3, c0:c0 + CW])
        cb = _dot(u_main, w(COL_CB + c0))
        cz = _dot(u_main, w(COL_CZ + c0))
        y_scr[:, c0:c0 + CW] = (cb * conv * (cz * _sigmoid(cz))).astype(BF16)
        decay_block(j)

    dt_ref[0, 0] = drow_scr[...].T[:, :D_COLS]

    for j in range(D_MODEL // CW):
        c0 = j * CW
        pc = _dot(y_scr[...], woc_ref[:, c0:c0 + CW])
        ma = _dot(u_main, wm_ref[:, c0:c0 + CW])
        a_ref[0, :, c0:c0 + CW] = (_sigmoid(ma) * pc).astype(BF16)
        mb = _dot(u_main, wm_ref[:, D_MODEL + c0:D_MODEL + c0 + CW])
        smb_ref[0, :, c0:c0 + CW] = _sigmoid(mb).astype(BF16)
        r = _dot(u_main, w(COL_R + c0))
        rs_ref[0, :, c0:c0 + CW] = (r * _sigmoid(r)).astype(BF16)
        v_ref[0, :, c0:c0 + CW] = _dot(u_main, w(COL_V + c0)).astype(BF16)


def _gla_kernel(qf_ref, kft_ref, qb_ref, kbt_ref, v_ref, rs_ref, smb_ref, a_ref, x_ref,
                dt_ref, s0_ref, glag_ref, wog_ref, wo_ref, gpost_ref,
                out_ref, state_scr, ob_scr, vseq_scr, o_scr, sc_scr, kv_scr, sb_scr):
    p = pl.program_id(1)
    i = pl.program_id(2)
    nt = pl.num_programs(2)
    t = TILE

    row = lax.broadcasted_iota(jnp.int32, (BLOCK, BLOCK), 0)
    col = lax.broadcasted_iota(jnp.int32, (BLOCK, BLOCK), 1)

    def rows(c):
        return slice(c * BLOCK, (c + 1) * BLOCK)

    def hk(h):
        return slice(h * HEAD_K, (h + 1) * HEAD_K)

    def hv(h):
        return slice(h * HEAD_V, (h + 1) * HEAD_V)

    def decay(r, h):
        return dt_ref[0, 0, hk(h), r:r + 1]

    def scan_tile(q_ref, kt_ref, v_at, d_row0, mask, order, emit):
        def products(h):
            for c in range(BPT):
                kt = kt_ref[0, hk(h), rows(c)]
                sc = _dot(q_ref[0, rows(c), hk(h)], kt)
                sc_scr[c, h] = jnp.where(mask, sc, 0.0).astype(BF16)
                kv_scr[c, h] = _dot(kt, v_at(c, h))

        def carry(h):
            state = state_scr[h]
            for c in order:
                s_in = state * decay(d_row0 + c, h)
                sb_scr[c, h] = s_in.astype(BF16)
                state = (s_in + kv_scr[c, h]) * decay(d_row0 + BPT + c, h)
            state_scr[h] = state

        def outputs(h):
            for c in range(BPT):
                lhs = jnp.concatenate([q_ref[0, rows(c), hk(h)], sc_scr[c, h]], axis=1)
                rhs = jnp.concatenate([sb_scr[c, h], v_at(c, h)], axis=0)
                emit(c, h, _dot(lhs, rhs))

        products(0)
        for h in range(HEADS):
            if h + 1 < HEADS:
                products(h + 1)
            carry(h)
            outputs(h)

    @pl.when(jnp.logical_and(p == 0, i == 0))
    def _():
        state_scr[...] = jnp.zeros_like(state_scr)

    @pl.when(jnp.logical_and(p == 1, i == 0))
    def _():
        state_scr[...] = s0_ref[...]

    @pl.when(p == 0)
    def _():
        base = (nt - 1 - i) * t

        def emit(c, h, o):
            ob_scr[pl.ds(pl.multiple_of(base + c * BLOCK, BLOCK), BLOCK), hv(h)] = o

        def v_at(c, h):
            return v_ref[0, rows(c), hv(h)]

        vseq_scr[pl.ds(pl.multiple_of(base, TILE), TILE), :] = v_ref[0]
        scan_tile(qb_ref, kbt_ref, v_at, 2 * BPT, col > row, list(reversed(range(BPT))), emit)

    @pl.when(p == 1)
    def _():
        base = i * t

        def emit(c, h, o):
            ob = ob_scr[pl.ds(pl.multiple_of(base + c * BLOCK, BLOCK), BLOCK), hv(h)]
            o_scr[rows(c), hv(h)] = o + ob

        def v_at(c, h):
            return vseq_scr[pl.ds(pl.multiple_of(base + c * BLOCK, BLOCK), BLOCK), hv(h)]

        scan_tile(qf_ref, kft_ref, v_at, 0, col <= row, list(range(BPT)), emit)
        glag = glag_ref[...]
        for r in range(2):
            rsl = slice(r * (t // 2), (r + 1) * (t // 2))
            ys = []
            for h in range(HEADS):
                oh = _rms(o_scr[rsl, hv(h)], glag)
                ys.append((oh * rs_ref[0, rsl, hv(h)].astype(F32)).astype(BF16))
            p_gla = _dot(jnp.concatenate(ys, axis=-1), wog_ref[...])
            merged = a_ref[0, rsl, :].astype(F32) + smb_ref[0, rsl, :].astype(F32) * p_gla
            out = _dot(merged.astype(BF16), wo_ref[...])
            out_ref[0, rsl, :] = x_ref[0, rsl, :] + _rms(out, gpost_ref[...])


def kernel(x, meta_tokens, norm_pre, w_in, conv_w, w_gate_fwd, b_gate_fwd, w_gate_bwd,
           b_gate_bwd, gla_norm, w_out_conv, w_out_gla, w_merge_out, norm_post):
    bsz, seq, _ = x.shape
    assert seq % TILE == 0 and norm_pre.shape[0] == 1
    nt = seq // TILE

    w_main = w_in[0].astype(BF16)
    w_merge = w_main[:, COL_MERGE:]
    w_lr = jnp.pad(w_main[:, COL_LR:COL_MERGE], ((0, 0), (0, LR_PAD - 2 * GATE_RANK)))
    wg = jnp.zeros((LR_PAD, 2 * DK), F32)
    wg = wg.at[:GATE_RANK, :DK].set(w_gate_fwd[0])
    wg = wg.at[GATE_RANK:2 * GATE_RANK, DK:].set(w_gate_bwd[0]).astype(BF16)
    bg = jnp.concatenate([b_gate_fwd[0], b_gate_bwd[0]])[None, :]
    woc = w_out_conv[0].astype(BF16)
    wog = w_out_gla[0].astype(BF16)
    wo = w_merge_out[0].astype(BF16)
    gpre = norm_pre[0][None, :]
    gpost = norm_post[0][None, :]
    glag = gla_norm[0][None, :]
    convw = conv_w[0]

    def col_spec(width, col):
        return pl.BlockSpec((D_MODEL, width), lambda g, c=col // width: (0, c))

    def full1(shape):
        return pl.BlockSpec(shape, lambda g: (0,) * len(shape))

    smeta, s0 = pl.pallas_call(
        _meta_kernel,
        grid=(1,),
        in_specs=[full1((N_META, D_MODEL)), full1((1, D_MODEL)),
                  col_spec(1024, COL_CC), col_spec(1024, COL_CX),
                  col_spec(DK, COL_K), col_spec(DV, COL_V), full1((D_MODEL, LR_PAD)),
                  full1((LR_PAD, 2 * DK)), full1((1, 2 * DK))],
        out_specs=[full1((N_META, D_MODEL)), full1((HEADS, HEAD_K, HEAD_V))],
        out_shape=[jax.ShapeDtypeStruct((N_META, D_MODEL), F32),
                   jax.ShapeDtypeStruct((HEADS, HEAD_K, HEAD_V), F32)],
        compiler_params=pltpu.CompilerParams(vmem_limit_bytes=VMEM_LIMIT),
        name="meta_prologue",
    )(meta_tokens, gpre, w_main, w_main, w_main, w_main, w_lr, wg, bg)

    hb = TILE // HALO
    n_hb = seq // HALO

    def const2(shape):
        return pl.BlockSpec(shape, lambda b, i: (0,) * len(shape),
                            pipeline_mode=pl.Buffered(1))

    def tok_spec(width):
        return pl.BlockSpec((1, TILE, width), lambda b, i: (b, i, 0))

    kt_spec = pl.BlockSpec((1, DK, TILE), lambda b, i: (b, 0, i))
    d_spec = pl.BlockSpec((1, 1, DK, D_COLS), lambda b, i: (b, i, 0, 0))
    tok_shape = lambda width: jax.ShapeDtypeStruct((bsz, seq, width), BF16)
    kt_shape = jax.ShapeDtypeStruct((bsz, DK, seq), BF16)
    d_shape = jax.ShapeDtypeStruct((bsz, nt, DK, D_COLS), F32)

    qf, kft, qb, kbt, v, rs, smb, a, dt = pl.pallas_call(
        _proj_kernel,
        grid=(bsz, nt),
        in_specs=[
            tok_spec(D_MODEL),
            pl.BlockSpec((1, HALO, D_MODEL),
                         lambda b, i: (b, jnp.maximum(i * hb - 1, 0), 0)),
            pl.BlockSpec((1, HALO, D_MODEL),
                         lambda b, i: (b, jnp.minimum((i + 1) * hb, n_hb - 1), 0)),
            const2((N_META, D_MODEL)), const2((1, D_MODEL)),
            const2((D_MODEL, N_IN)), const2((D_MODEL, 2 * D_MODEL)),
            const2((D_MODEL, LR_PAD)),
            const2((3, D_MODEL)), const2((LR_PAD, 2 * DK)), const2((1, 2 * DK)),
            const2((D_MODEL, D_MODEL)),
        ],
        out_specs=[tok_spec(DK), kt_spec, tok_spec(DK), kt_spec,
                   tok_spec(DV), tok_spec(DV), tok_spec(D_MODEL), tok_spec(D_MODEL),
                   d_spec],
        out_shape=[tok_shape(DK), kt_shape, tok_shape(DK), kt_shape,
                   tok_shape(DV), tok_shape(DV), tok_shape(D_MODEL), tok_shape(D_MODEL),
                   d_shape],
        scratch_shapes=[pltpu.VMEM((TILE + 2 * HALO, D_MODEL), BF16),
                        pltpu.VMEM((TILE + 2 * HALO, CW), F32),
                        pltpu.VMEM((TILE, D_MODEL), BF16),
                        pltpu.VMEM((TILE, 2 * DK), F32),
                        pltpu.VMEM((TILE, DK), F32),
                        pltpu.VMEM((TILE, DK), F32),
                        pltpu.VMEM((D_ROWS, DK), F32)],
        compiler_params=pltpu.CompilerParams(
            dimension_semantics=("parallel", "arbitrary"),
            vmem_limit_bytes=VMEM_LIMIT),
        name="inproj_conv_gates",
    )(x, x, x, smeta, gpre, w_main, w_merge, w_lr, convw, wg, bg, woc)

    def fwd_map(b, p, i):
        return (b, p * i, 0)

    def bwd_map(b, p, i):
        return (b, (1 - p) * (nt - 1 - i), 0)

    def tok2(width, imap):
        return pl.BlockSpec((1, TILE, width), imap)

    def const3(shape):
        return pl.BlockSpec(shape, lambda b, p, i: (0,) * len(shape))

    def kt2(imap):
        return pl.BlockSpec((1, DK, TILE), lambda b, p, i: (b, 0, imap(b, p, i)[1]))

    d_spec2 = pl.BlockSpec((1, 1, DK, D_COLS),
                           lambda b, p, i: (b, p * i + (1 - p) * (nt - 1 - i), 0, 0))

    out = pl.pallas_call(
        _gla_kernel,
        grid=(bsz, 2, nt),
        in_specs=[
            tok2(DK, fwd_map), kt2(fwd_map), tok2(DK, bwd_map), kt2(bwd_map),
            tok2(DV, bwd_map), tok2(DV, fwd_map), tok2(D_MODEL, fwd_map),
            tok2(D_MODEL, fwd_map), tok2(D_MODEL, fwd_map),
            d_spec2,
            const3((HEADS, HEAD_K, HEAD_V)), const3((1, HEAD_V)),
            const3((DV, D_MODEL)), const3((D_MODEL, D_MODEL)), const3((1, D_MODEL)),
        ],
        out_specs=tok2(D_MODEL, fwd_map),
        out_shape=jax.ShapeDtypeStruct((bsz, seq, D_MODEL), x.dtype),
        scratch_shapes=[pltpu.VMEM((HEADS, HEAD_K, HEAD_V), F32),
                        pltpu.VMEM((seq, DV), F32),
                        pltpu.VMEM((seq, DV), BF16),
                        pltpu.VMEM((TILE, DV), F32),
                        pltpu.VMEM((BPT, HEADS, BLOCK, BLOCK), BF16),
                        pltpu.VMEM((BPT, HEADS, HEAD_K, HEAD_V), F32),
                        pltpu.VMEM((BPT, HEADS, HEAD_K, HEAD_V), BF16)],
        compiler_params=pltpu.CompilerParams(
            dimension_semantics=("parallel", "arbitrary", "arbitrary"),
            vmem_limit_bytes=VMEM_LIMIT),
        name="gla_merge_out",
    )(qf, kft, qb, kbt, v, rs, smb, a, x, dt, s0, glag, wog, wo, gpost)
    return out
```

```python
import jax
import jax.numpy as jnp
from jax import lax
from jax.experimental import pallas as pl
from jax.experimental.pallas import tpu as pltpu

D_MODEL = 1024
N_META = 16
HEADS = 4
DK = 512
DV = 1024
HEAD_K = DK // HEADS
HEAD_V = DV // HEADS
GATE_RANK = 16
GATE_NORMALIZER = 16.0
EPS = 1e-6

COL_CB, COL_CC, COL_CX, COL_CZ = 0, 1024, 2048, 3072
COL_Q, COL_K, COL_V, COL_R = 4096, 4608, 5120, 6144
COL_LR = 7168
COL_MERGE = COL_LR + 2 * GATE_RANK
N_IN = COL_MERGE + 2 * D_MODEL
LR_PAD = 128

HALO = 16
TILE = 512
BLOCK = 128
BPT = TILE // BLOCK
CW = 256
VMEM_LIMIT = 56 * 1024 * 1024

F32 = jnp.float32
BF16 = jnp.bfloat16


def _dot(a, b):
    return jnp.dot(a, b, preferred_element_type=F32)


def _dot_nt(a, b):
    return lax.dot_general(a, b, (((1,), (1,)), ((), ())), preferred_element_type=F32)


def _dot_tn(a, b):
    return lax.dot_general(a, b, (((0,), (0,)), ((), ())), preferred_element_type=F32)


def _rms(x, g):
    ms = jnp.mean(x * x, axis=-1, keepdims=True)
    return x * lax.rsqrt(ms + EPS) * g


def _sigmoid(x):
    return 1.0 / (1.0 + jnp.exp(-x))


def _log_sigmoid(x):
    return jnp.minimum(x, 0.0) - jnp.log1p(jnp.exp(-jnp.abs(x)))


def _sum_matrix(n, ref, reverse):
    r = lax.broadcasted_iota(jnp.int32, (n, 2 * n), 0)
    c = lax.broadcasted_iota(jnp.int32, (n, 2 * n), 1)
    c = jnp.where(c >= n, c - n, c)
    if reverse:
        m = jnp.where(c >= r, 1.0, 0.0) - jnp.where(c >= ref, 1.0, 0.0)
    else:
        m = jnp.where(c <= r, 1.0, 0.0) - jnp.where(c < ref, 1.0, 0.0)
    return m.astype(BF16)


def _row_sums(mat2, g):
    hi = g.astype(BF16)
    lo = (g - hi.astype(F32)).astype(BF16)
    return _dot(mat2, jnp.concatenate([hi, lo], axis=0))


def _meta_kernel(meta_ref, gpre_ref, wcc_ref, wcx_ref, wk_ref, wv_ref, wlr_ref,
                 wg_ref, bg_ref, smeta_ref, s0_ref):
    u = _rms(meta_ref[...], gpre_ref[...]).astype(BF16)
    smeta_ref[...] = _dot(u, wcc_ref[...]) * _dot(u, wcx_ref[...])
    k = _dot(u, wk_ref[...])
    v = _dot(u, wv_ref[...]).astype(BF16)
    lr = _dot(u, wlr_ref[...]).astype(BF16)
    z = _dot(lr, wg_ref[:, :DK]) + bg_ref[:, :DK]
    g = _log_sigmoid(z) * (1.0 / GATE_NORMALIZER)
    b = _row_sums(_sum_matrix(N_META, 0, reverse=False), g)
    kdec = (k * jnp.exp(b[N_META - 1:N_META, :] - b)).astype(BF16)
    for h in range(HEADS):
        s0_ref[h] = _dot_tn(kdec[:, h * HEAD_K:(h + 1) * HEAD_K],
                            v[:, h * HEAD_V:(h + 1) * HEAD_V])


def _proj_kernel(x_ref, xp_ref, xn_ref, smeta_ref, gpre_ref, w_ref, wm_ref, wlr_ref,
                 convw_ref, wg_ref, bg_ref, woc_ref,
                 qf_ref, kf_ref, qb_ref, kb_ref, v_ref, rs_ref, smb_ref, a_ref, d_ref,
                 u_scr, s_scr, y_scr, g_scr, q_scr, k_scr):
    i = pl.program_id(1)
    last = pl.num_programs(1) - 1
    t = TILE
    gpre = gpre_ref[...]

    u_scr[0:HALO, :] = _rms(xp_ref[0], gpre).astype(BF16)
    u_scr[HALO:HALO + t, :] = _rms(x_ref[0], gpre).astype(BF16)
    u_scr[HALO + t:, :] = _rms(xn_ref[0], gpre).astype(BF16)

    def w(col, width=CW):
        return w_ref[:, col:col + width]

    u_main = u_scr[HALO:HALO + t, :]
    lr = _dot(u_main, wlr_ref[...]).astype(BF16)
    g_scr[...] = _log_sigmoid(_dot(lr, wg_ref[...]) + bg_ref[...]) * (1.0 / GATE_NORMALIZER)
    q_scr[...] = _dot(u_main, w(COL_Q, DK)) * (HEAD_K ** -0.5)
    k_scr[...] = _dot(u_main, w(COL_K, DK))

    half = BLOCK // 2
    mat_f = _sum_matrix(BLOCK, half, reverse=False)
    mat_b = _sum_matrix(BLOCK, half, reverse=True)

    def decay_block(c):
        rs = slice(c * BLOCK, (c + 1) * BLOCK)
        qc = q_scr[rs, :]
        kc = k_scr[rs, :]
        gf = g_scr[rs, :DK]
        gb = g_scr[rs, DK:]
        bf = _row_sums(mat_f, gf)
        bb = _row_sums(mat_b, gb)
        qf_ref[0, rs, :] = (qc * jnp.exp(bf)).astype(BF16)
        kf_ref[0, rs, :] = (kc * jnp.exp(-bf)).astype(BF16)
        qb_ref[0, rs, :] = (qc * jnp.exp(bb)).astype(BF16)
        kb_ref[0, rs, :] = (kc * jnp.exp(-bb)).astype(BF16)
        d_ref[0, 0, c:c + 1, :] = jnp.exp(gf[0:1, :] - bf[0:1, :])
        d_ref[0, 0, BPT + c:BPT + c + 1, :] = jnp.exp(bf[BLOCK - 1:BLOCK, :])
        d_ref[0, 0, 2 * BPT + c:2 * BPT + c + 1, :] = jnp.exp(
            gb[BLOCK - 1:BLOCK, :] - bb[BLOCK - 1:BLOCK, :])
        d_ref[0, 0, 3 * BPT + c:3 * BPT + c + 1, :] = jnp.exp(bb[0:1, :])

    assert D_MODEL // CW == BPT
    for j in range(D_MODEL // CW):
        c0 = j * CW
        u_ext = u_scr[...]
        s_scr[...] = _dot(u_ext, w(COL_CC + c0)) * _dot(u_ext, w(COL_CX + c0))
        s_scr[HALO - 1:HALO, :] = jnp.where(
            i == 0, smeta_ref[N_META - 1:N_META, c0:c0 + CW], s_scr[HALO - 1:HALO, :])
        s_scr[HALO + t:HALO + t + 1, :] = jnp.where(
            i == last, 0.0, s_scr[HALO + t:HALO + t + 1, :])
        conv = (s_scr[HALO - 1:HALO - 1 + t, :] * convw_ref[0:1, c0:c0 + CW]
                + s_scr[HALO:HALO + t, :] * convw_ref[1:2, c0:c0 + CW]
                + s_scr[HALO + 1:HALO + 1 + t, :] * convw_ref[2:3, c0:c0 + CW])
        cb = _dot(u_main, w(COL_CB + c0))
        cz = _dot(u_main, w(COL_CZ + c0))
        y_scr[:, c0:c0 + CW] = (cb * conv * (cz * _sigmoid(cz))).astype(BF16)
        decay_block(j)

    for j in range(D_MODEL // CW):
        c0 = j * CW
        pc = _dot(y_scr[...], woc_ref[:, c0:c0 + CW])
        ma = _dot(u_main, wm_ref[:, c0:c0 + CW])
        a_ref[0, :, c0:c0 + CW] = (_sigmoid(ma) * pc).astype(BF16)
        mb = _dot(u_main, wm_ref[:, D_MODEL + c0:D_MODEL + c0 + CW])
        smb_ref[0, :, c0:c0 + CW] = _sigmoid(mb).astype(BF16)
        r = _dot(u_main, w(COL_R + c0))
        rs_ref[0, :, c0:c0 + CW] = (r * _sigmoid(r)).astype(BF16)
        v_ref[0, :, c0:c0 + CW] = _dot(u_main, w(COL_V + c0)).astype(BF16)


def _gla_kernel(qf_ref, kf_ref, qb_ref, kb_ref, v_ref, rs_ref, smb_ref, a_ref, x_ref,
                d_ref, s0_ref, glag_ref, wog_ref, wo_ref, gpost_ref,
                out_ref, state_scr, ob_scr, o_scr, sc_scr, kv_scr, sb_scr):
    j = pl.program_id(1)
    nt = ob_scr.shape[0] // TILE

    row = lax.broadcasted_iota(jnp.int32, (BLOCK, BLOCK), 0)
    col = lax.broadcasted_iota(jnp.int32, (BLOCK, BLOCK), 1)
    eye = row == col

    def rows(c):
        return slice(c * BLOCK, (c + 1) * BLOCK)

    def hk(h):
        return slice(h * HEAD_K, (h + 1) * HEAD_K)

    def hv(h):
        return slice(h * HEAD_V, (h + 1) * HEAD_V)

    def as_column(d):
        return jnp.sum(jnp.where(eye, d, 0.0), axis=1, keepdims=True)

    def scan_tile(q_at, k_at, v_at, d_at, mask, order, emit):
        for c in range(BPT):
            for h in range(HEADS):
                k = k_at(c, h)
                sc_scr[c, h] = jnp.where(mask, _dot_nt(q_at(c, h), k), 0.0).astype(BF16)
                kv_scr[c, h] = _dot_tn(k, v_at(c, h))
        for h in range(HEADS):
            state = state_scr[h]
            for c in order:
                s_in = state * as_column(d_at(c, h))
                sb_scr[c, h] = s_in.astype(BF16)
                state = (s_in + kv_scr[c, h]) * as_column(d_at(BPT + c, h))
            state_scr[h] = state
        for c in range(BPT):
            for h in range(HEADS):
                lhs = jnp.concatenate([q_at(c, h), sc_scr[c, h]], axis=1)
                rhs = jnp.concatenate([sb_scr[c, h], v_at(c, h)], axis=0)
                emit(c, h, _dot(lhs, rhs))

    @pl.when(j == 0)
    def _():
        state_scr[...] = jnp.zeros_like(state_scr)

        def one_tile(n, carry):
            tile = nt - 1 - n

            def seq_rows(c):
                return pl.ds(pl.multiple_of(tile * TILE + c * BLOCK, BLOCK), BLOCK)

            def emit(c, h, o):
                ob_scr[seq_rows(c), hv(h)] = o

            scan_tile(lambda c, h: qb_ref[0, seq_rows(c), hk(h)],
                      lambda c, h: kb_ref[0, seq_rows(c), hk(h)],
                      lambda c, h: v_ref[0, seq_rows(c), hv(h)],
                      lambda r, h: d_ref[0, tile, 2 * BPT + r:2 * BPT + r + 1, hk(h)],
                      col > row, list(reversed(range(BPT))), emit)
            return carry

        lax.fori_loop(0, nt, one_tile, 0)

    @pl.when(j == 1)
    def _():
        state_scr[...] = s0_ref[...]

    @pl.when(j >= 1)
    def _():
        i = j - 1
        base = i * TILE

        def seq_rows(c):
            return pl.ds(pl.multiple_of(base + c * BLOCK, BLOCK), BLOCK)

        def emit(c, h, o):
            o_scr[rows(c), hv(h)] = o + ob_scr[seq_rows(c), hv(h)]

        scan_tile(lambda c, h: qf_ref[0, rows(c), hk(h)],
                  lambda c, h: kf_ref[0, rows(c), hk(h)],
                  lambda c, h: v_ref[0, seq_rows(c), hv(h)],
                  lambda r, h: d_ref[0, i, r:r + 1, hk(h)],
                  col <= row, list(range(BPT)), emit)
        glag = glag_ref[...]
        ys = []
        for h in range(HEADS):
            oh = _rms(o_scr[:, hv(h)], glag)
            ys.append((oh * rs_ref[0, :, hv(h)].astype(F32)).astype(BF16))
        y = jnp.concatenate(ys, axis=-1)
        p_gla = _dot(y, wog_ref[...])
        merged = a_ref[0].astype(F32) + smb_ref[0].astype(F32) * p_gla
        out = _dot(merged.astype(BF16), wo_ref[...])
        out_ref[0] = x_ref[0] + _rms(out, gpost_ref[...])


def kernel(x, meta_tokens, norm_pre, w_in, conv_w, w_gate_fwd, b_gate_fwd, w_gate_bwd,
           b_gate_bwd, gla_norm, w_out_conv, w_out_gla, w_merge_out, norm_post):
    bsz, seq, _ = x.shape
    assert seq % TILE == 0 and norm_pre.shape[0] == 1
    nt = seq // TILE

    w_main = w_in[0].astype(BF16)
    w_merge = w_main[:, COL_MERGE:]
    w_lr = jnp.pad(w_main[:, COL_LR:COL_MERGE], ((0, 0), (0, LR_PAD - 2 * GATE_RANK)))
    wg = jnp.zeros((LR_PAD, 2 * DK), F32)
    wg = wg.at[:GATE_RANK, :DK].set(w_gate_fwd[0])
    wg = wg.at[GATE_RANK:2 * GATE_RANK, DK:].set(w_gate_bwd[0]).astype(BF16)
    bg = jnp.concatenate([b_gate_fwd[0], b_gate_bwd[0]])[None, :]
    woc = w_out_conv[0].astype(BF16)
    wog = w_out_gla[0].astype(BF16)
    wo = w_merge_out[0].astype(BF16)
    gpre = norm_pre[0][None, :]
    gpost = norm_post[0][None, :]
    glag = gla_norm[0][None, :]
    convw = conv_w[0]

    def col_spec(width, col):
        return pl.BlockSpec((D_MODEL, width), lambda g, c=col // width: (0, c))

    def full1(shape):
        return pl.BlockSpec(shape, lambda g: (0,) * len(shape))

    smeta, s0 = pl.pallas_call(
        _meta_kernel,
        grid=(1,),
        in_specs=[full1((N_META, D_MODEL)), full1((1, D_MODEL)),
                  col_spec(1024, COL_CC), col_spec(1024, COL_CX),
                  col_spec(DK, COL_K), col_spec(DV, COL_V), full1((D_MODEL, LR_PAD)),
                  full1((LR_PAD, 2 * DK)), full1((1, 2 * DK))],
        out_specs=[full1((N_META, D_MODEL)), full1((HEADS, HEAD_K, HEAD_V))],
        out_shape=[jax.ShapeDtypeStruct((N_META, D_MODEL), F32),
                   jax.ShapeDtypeStruct((HEADS, HEAD_K, HEAD_V), F32)],
        compiler_params=pltpu.CompilerParams(vmem_limit_bytes=VMEM_LIMIT),
        name="meta_prologue",
    )(meta_tokens, gpre, w_main, w_main, w_main, w_main, w_lr, wg, bg)

    hb = TILE // HALO
    n_hb = seq // HALO

    def const2(shape):
        return pl.BlockSpec(shape, lambda b, i: (0,) * len(shape),
                            pipeline_mode=pl.Buffered(1))

    def tok_spec(width):
        return pl.BlockSpec((1, TILE, width), lambda b, i: (b, i, 0))

    d_spec = pl.BlockSpec((1, 1, 4 * BPT, DK), lambda b, i: (b, i, 0, 0))
    tok_shape = lambda width: jax.ShapeDtypeStruct((bsz, seq, width), BF16)
    d_shape = jax.ShapeDtypeStruct((bsz, nt, 4 * BPT, DK), F32)

    qf, kf, qb, kb, v, rs, smb, a, d = pl.pallas_call(
        _proj_kernel,
        grid=(bsz, nt),
        in_specs=[
            tok_spec(D_MODEL),
            pl.BlockSpec((1, HALO, D_MODEL),
                         lambda b, i: (b, jnp.maximum(i * hb - 1, 0), 0)),
            pl.BlockSpec((1, HALO, D_MODEL),
                         lambda b, i: (b, jnp.minimum((i + 1) * hb, n_hb - 1), 0)),
            const2((N_META, D_MODEL)), const2((1, D_MODEL)),
            const2((D_MODEL, N_IN)), const2((D_MODEL, 2 * D_MODEL)),
            const2((D_MODEL, LR_PAD)),
            const2((3, D_MODEL)), const2((LR_PAD, 2 * DK)), const2((1, 2 * DK)),
            const2((D_MODEL, D_MODEL)),
        ],
        out_specs=[tok_spec(DK), tok_spec(DK), tok_spec(DK), tok_spec(DK),
                   tok_spec(DV), tok_spec(DV), tok_spec(D_MODEL), tok_spec(D_MODEL),
                   d_spec],
        out_shape=[tok_shape(DK), tok_shape(DK), tok_shape(DK), tok_shape(DK),
                   tok_shape(DV), tok_shape(DV), tok_shape(D_MODEL), tok_shape(D_MODEL),
                   d_shape],
        scratch_shapes=[pltpu.VMEM((TILE + 2 * HALO, D_MODEL), BF16),
                        pltpu.VMEM((TILE + 2 * HALO, CW), F32),
                        pltpu.VMEM((TILE, D_MODEL), BF16),
                        pltpu.VMEM((TILE, 2 * DK), F32),
                        pltpu.VMEM((TILE, DK), F32),
                        pltpu.VMEM((TILE, DK), F32)],
        compiler_params=pltpu.CompilerParams(
            dimension_semantics=("parallel", "arbitrary"),
            vmem_limit_bytes=VMEM_LIMIT),
        name="inproj_conv_gates",
    )(x, x, x, smeta, gpre, w_main, w_merge, w_lr, convw, wg, bg, woc)

    def tile_map(b, j):
        return (b, jnp.maximum(j - 1, 0), 0)

    def seq_map(b, j):
        return (b, 0, 0)

    def tok2(width):
        return pl.BlockSpec((1, TILE, width), tile_map)

    def seq2(width):
        return pl.BlockSpec((1, seq, width), seq_map)

    def const2b(shape):
        return pl.BlockSpec(shape, lambda b, j: (0,) * len(shape))

    out = pl.pallas_call(
        _gla_kernel,
        grid=(bsz, nt + 1),
        in_specs=[
            tok2(DK), tok2(DK), seq2(DK), seq2(DK), seq2(DV),
            tok2(DV), tok2(D_MODEL), tok2(D_MODEL), tok2(D_MODEL),
            pl.BlockSpec((1, nt, 4 * BPT, DK), lambda b, j: (b, 0, 0, 0)),
            const2b((HEADS, HEAD_K, HEAD_V)), const2b((1, HEAD_V)),
            const2b((DV, D_MODEL)), const2b((D_MODEL, D_MODEL)), const2b((1, D_MODEL)),
        ],
        out_specs=tok2(D_MODEL),
        out_shape=jax.ShapeDtypeStruct((bsz, seq, D_MODEL), x.dtype),
        scratch_shapes=[pltpu.VMEM((HEADS, HEAD_K, HEAD_V), F32),
                        pltpu.VMEM((seq, DV), F32),
                        pltpu.VMEM((TILE, DV), F32),
                        pltpu.VMEM((BPT, HEADS, BLOCK, BLOCK), BF16),
                        pltpu.VMEM((BPT, HEADS, HEAD_K, HEAD_V), F32),
                        pltpu.VMEM((BPT, HEADS, HEAD_K, HEAD_V), BF16)],
        compiler_params=pltpu.CompilerParams(
            dimension_semantics=("parallel", "arbitrary"),
            vmem_limit_bytes=VMEM_LIMIT),
        name="gla_merge_out",
    )(qf, kf, qb, kb, v, rs, smb, a, x, d, s0, glag, wog, wo, gpost)
    return out
```

```python
import jax
import jax.numpy as jnp
from jax import lax
from jax.experimental import pallas as pl
from jax.experimental.pallas import tpu as pltpu

D_MODEL = 1024
N_META = 16
HEADS = 4
DK = 512
DV = 1024
HEAD_K = DK // HEADS
HEAD_V = DV // HEADS
GATE_RANK = 16
GATE_NORMALIZER = 16.0
EPS = 1e-6

COL_CB, COL_CC, COL_CX, COL_CZ = 0, 1024, 2048, 3072
COL_Q, COL_K, COL_V, COL_R = 4096, 4608, 5120, 6144
COL_LR = 7168
COL_MERGE = COL_LR + 2 * GATE_RANK
N_IN = COL_MERGE + 2 * D_MODEL
LR_PAD = 128

HALO = 16
TILE = 512
BLOCK = 128
BPT = TILE // BLOCK
CW = 256
VMEM_LIMIT = 56 * 1024 * 1024

F32 = jnp.float32
BF16 = jnp.bfloat16


def _dot(a, b):
    return jnp.dot(a, b, preferred_element_type=F32)


def _dot_nt(a, b):
    return lax.dot_general(a, b, (((1,), (1,)), ((), ())), preferred_element_type=F32)


def _dot_tn(a, b):
    return lax.dot_general(a, b, (((0,), (0,)), ((), ())), preferred_element_type=F32)


def _rms(x, g):
    ms = jnp.mean(x * x, axis=-1, keepdims=True)
    return x * lax.rsqrt(ms + EPS) * g


def _sigmoid(x):
    return 1.0 / (1.0 + jnp.exp(-x))


def _log_sigmoid(x):
    return jnp.minimum(x, 0.0) - jnp.log1p(jnp.exp(-jnp.abs(x)))


def _sum_matrix(n, ref, reverse):
    r = lax.broadcasted_iota(jnp.int32, (n, 2 * n), 0)
    c = lax.broadcasted_iota(jnp.int32, (n, 2 * n), 1)
    c = jnp.where(c >= n, c - n, c)
    if reverse:
        m = jnp.where(c >= r, 1.0, 0.0) - jnp.where(c >= ref, 1.0, 0.0)
    else:
        m = jnp.where(c <= r, 1.0, 0.0) - jnp.where(c < ref, 1.0, 0.0)
    return m.astype(BF16)


def _row_sums(mat2, g):
    hi = g.astype(BF16)
    lo = (g - hi.astype(F32)).astype(BF16)
    return _dot(mat2, jnp.concatenate([hi, lo], axis=0))


def _meta_kernel(meta_ref, gpre_ref, wcc_ref, wcx_ref, wk_ref, wv_ref, wlr_ref,
                 wg_ref, bg_ref, smeta_ref, s0_ref):
    u = _rms(meta_ref[...], gpre_ref[...]).astype(BF16)
    smeta_ref[...] = _dot(u, wcc_ref[...]) * _dot(u, wcx_ref[...])
    k = _dot(u, wk_ref[...])
    v = _dot(u, wv_ref[...]).astype(BF16)
    lr = _dot(u, wlr_ref[...]).astype(BF16)
    z = _dot(lr, wg_ref[:, :DK]) + bg_ref[:, :DK]
    g = _log_sigmoid(z) * (1.0 / GATE_NORMALIZER)
    b = _row_sums(_sum_matrix(N_META, 0, reverse=False), g)
    kdec = (k * jnp.exp(b[N_META - 1:N_META, :] - b)).astype(BF16)
    for h in range(HEADS):
        s0_ref[h] = _dot_tn(kdec[:, h * HEAD_K:(h + 1) * HEAD_K],
                            v[:, h * HEAD_V:(h + 1) * HEAD_V])


def _proj_kernel(x_ref, xp_ref, xn_ref, smeta_ref, gpre_ref, w_ref, wm_ref, wlr_ref,
                 convw_ref, wg_ref, bg_ref, woc_ref,
                 qf_ref, kf_ref, qb_ref, kb_ref, v_ref, rs_ref, smb_ref, a_ref, d_ref,
                 u_scr, xg_scr, s_scr, y_scr, g_scr, q_scr, k_scr):
    i = pl.program_id(1)
    last = pl.num_programs(1) - 1
    t = TILE
    gpre = gpre_ref[...]

    def w(col, width=CW):
        return w_ref[:, col:col + width]

    x_main = x_ref[0]
    xg_scr[...] = (x_main * gpre).astype(BF16)
    xg = xg_scr[...]
    lr_raw = _dot(xg, wlr_ref[...])
    q_raw = _dot(xg, w(COL_Q, DK))
    k_raw = _dot(xg, w(COL_K, DK))

    rinv = lax.rsqrt(jnp.mean(x_main * x_main, axis=-1, keepdims=True) + EPS)
    u_scr[0:HALO, :] = _rms(xp_ref[0], gpre).astype(BF16)
    u_scr[HALO:HALO + t, :] = (x_main * rinv * gpre).astype(BF16)
    u_scr[HALO + t:, :] = _rms(xn_ref[0], gpre).astype(BF16)

    u_main = u_scr[HALO:HALO + t, :]
    lr = (lr_raw * rinv).astype(BF16)
    g_scr[...] = _log_sigmoid(_dot(lr, wg_ref[...]) + bg_ref[...]) * (1.0 / GATE_NORMALIZER)
    q_scr[...] = q_raw * rinv * (HEAD_K ** -0.5)
    k_scr[...] = k_raw * rinv

    half = BLOCK // 2
    mat_f = _sum_matrix(BLOCK, half, reverse=False)
    mat_b = _sum_matrix(BLOCK, half, reverse=True)

    def decay_block(c):
        rs = slice(c * BLOCK, (c + 1) * BLOCK)
        qc = q_scr[rs, :]
        kc = k_scr[rs, :]
        gf = g_scr[rs, :DK]
        gb = g_scr[rs, DK:]
        bf = _row_sums(mat_f, gf)
        bb = _row_sums(mat_b, gb)
        qf_ref[0, rs, :] = (qc * jnp.exp(bf)).astype(BF16)
        kf_ref[0, rs, :] = (kc * jnp.exp(-bf)).astype(BF16)
        qb_ref[0, rs, :] = (qc * jnp.exp(bb)).astype(BF16)
        kb_ref[0, rs, :] = (kc * jnp.exp(-bb)).astype(BF16)
        d_ref[0, 0, c:c + 1, :] = jnp.exp(gf[0:1, :] - bf[0:1, :])
        d_ref[0, 0, BPT + c:BPT + c + 1, :] = jnp.exp(bf[BLOCK - 1:BLOCK, :])
        d_ref[0, 0, 2 * BPT + c:2 * BPT + c + 1, :] = jnp.exp(
            gb[BLOCK - 1:BLOCK, :] - bb[BLOCK - 1:BLOCK, :])
        d_ref[0, 0, 3 * BPT + c:3 * BPT + c + 1, :] = jnp.exp(bb[0:1, :])

    assert D_MODEL // CW == BPT
    for j in range(D_MODEL // CW):
        c0 = j * CW
        u_ext = u_scr[...]
        s_scr[...] = _dot(u_ext, w(COL_CC + c0)) * _dot(u_ext, w(COL_CX + c0))
        s_scr[HALO - 1:HALO, :] = jnp.where(
            i == 0, smeta_ref[N_META - 1:N_META, c0:c0 + CW], s_scr[HALO - 1:HALO, :])
        s_scr[HALO + t:HALO + t + 1, :] = jnp.where(
            i == last, 0.0, s_scr[HALO + t:HALO + t + 1, :])
        conv = (s_scr[HALO - 1:HALO - 1 + t, :] * convw_ref[0:1, c0:c0 + CW]
                + s_scr[HALO:HALO + t, :] * convw_ref[1:2, c0:c0 + CW]
                + s_scr[HALO + 1:HALO + 1 + t, :] * convw_ref[2:3, c0:c0 + CW])
        cb = _dot(u_main, w(COL_CB + c0))
        cz = _dot(u_main, w(COL_CZ + c0))
        y_scr[:, c0:c0 + CW] = (cb * conv * (cz * _sigmoid(cz))).astype(BF16)
        decay_block(j)

    for j in range(D_MODEL // CW):
        c0 = j * CW
        pc = _dot(y_scr[...], woc_ref[:, c0:c0 + CW])
        ma = _dot(u_main, wm_ref[:, c0:c0 + CW])
        a_ref[0, :, c0:c0 + CW] = (_sigmoid(ma) * pc).astype(BF16)
        mb = _dot(u_main, wm_ref[:, D_MODEL + c0:D_MODEL + c0 + CW])
        smb_ref[0, :, c0:c0 + CW] = _sigmoid(mb).astype(BF16)
        r = _dot(u_main, w(COL_R + c0))
        rs_ref[0, :, c0:c0 + CW] = (r * _sigmoid(r)).astype(BF16)
        v_ref[0, :, c0:c0 + CW] = _dot(u_main, w(COL_V + c0)).astype(BF16)


def _gla_kernel(qf_ref, kf_ref, qb_ref, kb_ref, v_ref, rs_ref, smb_ref, a_ref, x_ref,
                d_ref, s0_ref, glag_ref, wog_ref, wo_ref, gpost_ref,
                out_ref, state_scr, ob_scr, o_scr, sc_scr, kv_scr, sb_scr):
    p = pl.program_id(1)
    i = pl.program_id(2)
    nt = pl.num_programs(2)
    t = TILE

    row = lax.broadcasted_iota(jnp.int32, (BLOCK, BLOCK), 0)
    col = lax.broadcasted_iota(jnp.int32, (BLOCK, BLOCK), 1)
    eye = row == col

    def rows(c):
        return slice(c * BLOCK, (c + 1) * BLOCK)

    def hk(h):
        return slice(h * HEAD_K, (h + 1) * HEAD_K)

    def hv(h):
        return slice(h * HEAD_V, (h + 1) * HEAD_V)

    def as_column(r, h):
        d = d_ref[0, 0, r:r + 1, hk(h)]
        return jnp.sum(jnp.where(eye, d, 0.0), axis=1, keepdims=True)

    def scan_tile(q_ref, k_ref, d_row0, mask, order, emit):
        for c in range(BPT):
            for h in range(HEADS):
                q = q_ref[0, rows(c), hk(h)]
                k = k_ref[0, rows(c), hk(h)]
                sc_scr[c, h] = jnp.where(mask, _dot_nt(q, k), 0.0).astype(BF16)
                kv_scr[c, h] = _dot_tn(k, v_ref[0, rows(c), hv(h)])
        for h in range(HEADS):
            state = state_scr[h]
            for c in order:
                s_in = state * as_column(d_row0 + c, h)
                sb_scr[c, h] = s_in.astype(BF16)
                state = (s_in + kv_scr[c, h]) * as_column(d_row0 + BPT + c, h)
            state_scr[h] = state
        for c in range(BPT):
            for h in range(HEADS):
                lhs = jnp.concatenate([q_ref[0, rows(c), hk(h)], sc_scr[c, h]], axis=1)
                rhs = jnp.concatenate([sb_scr[c, h], v_ref[0, rows(c), hv(h)]], axis=0)
                emit(c, h, _dot(lhs, rhs))

    @pl.when(jnp.logical_and(p == 0, i == 0))
    def _():
        state_scr[...] = jnp.zeros_like(state_scr)

    @pl.when(jnp.logical_and(p == 1, i == 0))
    def _():
        state_scr[...] = s0_ref[...]

    @pl.when(p == 0)
    def _():
        base = (nt - 1 - i) * t

        def emit(c, h, o):
            ob_scr[pl.ds(pl.multiple_of(base + c * BLOCK, BLOCK), BLOCK), hv(h)] = o

        scan_tile(qb_ref, kb_ref, 2 * BPT, col > row, list(reversed(range(BPT))), emit)

    @pl.when(p == 1)
    def _():
        base = i * t

        def emit(c, h, o):
            ob = ob_scr[pl.ds(pl.multiple_of(base + c * BLOCK, BLOCK), BLOCK), hv(h)]
            o_scr[rows(c), hv(h)] = o + ob

        scan_tile(qf_ref, kf_ref, 0, col <= row, list(range(BPT)), emit)
        glag = glag_ref[...]
        ys = []
        for h in range(HEADS):
            oh = _rms(o_scr[:, hv(h)], glag)
            ys.append((oh * rs_ref[0, :, hv(h)].astype(F32)).astype(BF16))
        y = jnp.concatenate(ys, axis=-1)
        p_gla = _dot(y, wog_ref[...])
        merged = a_ref[0].astype(F32) + smb_ref[0].astype(F32) * p_gla
        out = _dot(merged.astype(BF16), wo_ref[...])
        out_ref[0] = x_ref[0] + _rms(out, gpost_ref[...])


def kernel(x, meta_tokens, norm_pre, w_in, conv_w, w_gate_fwd, b_gate_fwd, w_gate_bwd,
           b_gate_bwd, gla_norm, w_out_conv, w_out_gla, w_merge_out, norm_post):
    bsz, seq, _ = x.shape
    assert seq % TILE == 0 and norm_pre.shape[0] == 1
    nt = seq // TILE

    w_main = w_in[0].astype(BF16)
    w_merge = w_main[:, COL_MERGE:]
    w_lr = jnp.pad(w_main[:, COL_LR:COL_MERGE], ((0, 0), (0, LR_PAD - 2 * GATE_RANK)))
    wg = jnp.zeros((LR_PAD, 2 * DK), F32)
    wg = wg.at[:GATE_RANK, :DK].set(w_gate_fwd[0])
    wg = wg.at[GATE_RANK:2 * GATE_RANK, DK:].set(w_gate_bwd[0]).astype(BF16)
    bg = jnp.concatenate([b_gate_fwd[0], b_gate_bwd[0]])[None, :]
    woc = w_out_conv[0].astype(BF16)
    wog = w_out_gla[0].astype(BF16)
    wo = w_merge_out[0].astype(BF16)
    gpre = norm_pre[0][None, :]
    gpost = norm_post[0][None, :]
    glag = gla_norm[0][None, :]
    convw = conv_w[0]

    def col_spec(width, col):
        return pl.BlockSpec((D_MODEL, width), lambda g, c=col // width: (0, c))

    def full1(shape):
        return pl.BlockSpec(shape, lambda g: (0,) * len(shape))

    smeta, s0 = pl.pallas_call(
        _meta_kernel,
        grid=(1,),
        in_specs=[full1((N_META, D_MODEL)), full1((1, D_MODEL)),
                  col_spec(1024, COL_CC), col_spec(1024, COL_CX),
                  col_spec(DK, COL_K), col_spec(DV, COL_V), full1((D_MODEL, LR_PAD)),
                  full1((LR_PAD, 2 * DK)), full1((1, 2 * DK))],
        out_specs=[full1((N_META, D_MODEL)), full1((HEADS, HEAD_K, HEAD_V))],
        out_shape=[jax.ShapeDtypeStruct((N_META, D_MODEL), F32),
                   jax.ShapeDtypeStruct((HEADS, HEAD_K, HEAD_V), F32)],
        compiler_params=pltpu.CompilerParams(vmem_limit_bytes=VMEM_LIMIT),
        name="meta_prologue",
    )(meta_tokens, gpre, w_main, w_main, w_main, w_main, w_lr, wg, bg)

    hb = TILE // HALO
    n_hb = seq // HALO

    def const2(shape):
        return pl.BlockSpec(shape, lambda b, i: (0,) * len(shape),
                            pipeline_mode=pl.Buffered(1))

    def tok_spec(width):
        return pl.BlockSpec((1, TILE, width), lambda b, i: (b, i, 0))

    d_spec = pl.BlockSpec((1, 1, 4 * BPT, DK), lambda b, i: (b, i, 0, 0))
    tok_shape = lambda width: jax.ShapeDtypeStruct((bsz, seq, width), BF16)
    d_shape = jax.ShapeDtypeStruct((bsz, nt, 4 * BPT, DK), F32)

    qf, kf, qb, kb, v, rs, smb, a, d = pl.pallas_call(
        _proj_kernel,
        grid=(bsz, nt),
        in_specs=[
            tok_spec(D_MODEL),
            pl.BlockSpec((1, HALO, D_MODEL),
                         lambda b, i: (b, jnp.maximum(i * hb - 1, 0), 0)),
            pl.BlockSpec((1, HALO, D_MODEL),
                         lambda b, i: (b, jnp.minimum((i + 1) * hb, n_hb - 1), 0)),
            const2((N_META, D_MODEL)), const2((1, D_MODEL)),
            const2((D_MODEL, N_IN)), const2((D_MODEL, 2 * D_MODEL)),
            const2((D_MODEL, LR_PAD)),
            const2((3, D_MODEL)), const2((LR_PAD, 2 * DK)), const2((1, 2 * DK)),
            const2((D_MODEL, D_MODEL)),
        ],
        out_specs=[tok_spec(DK), tok_spec(DK), tok_spec(DK), tok_spec(DK),
                   tok_spec(DV), tok_spec(DV), tok_spec(D_MODEL), tok_spec(D_MODEL),
                   d_spec],
        out_shape=[tok_shape(DK), tok_shape(DK), tok_shape(DK), tok_shape(DK),
                   tok_shape(DV), tok_shape(DV), tok_shape(D_MODEL), tok_shape(D_MODEL),
                   d_shape],
        scratch_shapes=[pltpu.VMEM((TILE + 2 * HALO, D_MODEL), BF16),
                        pltpu.VMEM((TILE, D_MODEL), BF16),
                        pltpu.VMEM((TILE + 2 * HALO, CW), F32),
                        pltpu.VMEM((TILE, D_MODEL), BF16),
                        pltpu.VMEM((TILE, 2 * DK), F32),
                        pltpu.VMEM((TILE, DK), F32),
                        pltpu.VMEM((TILE, DK), F32)],
        compiler_params=pltpu.CompilerParams(
            dimension_semantics=("parallel", "arbitrary"),
            vmem_limit_bytes=VMEM_LIMIT),
        name="inproj_conv_gates",
    )(x, x, x, smeta, gpre, w_main, w_merge, w_lr, convw, wg, bg, woc)

    def fwd_map(b, p, i):
        return (b, p * i, 0)

    def bwd_map(b, p, i):
        return (b, (1 - p) * (nt - 1 - i), 0)

    def both_map(b, p, i):
        return (b, p * i + (1 - p) * (nt - 1 - i), 0)

    def tok2(width, imap):
        return pl.BlockSpec((1, TILE, width), imap)

    def const3(shape):
        return pl.BlockSpec(shape, lambda b, p, i: (0,) * len(shape))

    d_spec2 = pl.BlockSpec((1, 1, 4 * BPT, DK),
                           lambda b, p, i: (b, p * i + (1 - p) * (nt - 1 - i), 0, 0))

    out = pl.pallas_call(
        _gla_kernel,
        grid=(bsz, 2, nt),
        in_specs=[
            tok2(DK, fwd_map), tok2(DK, fwd_map), tok2(DK, bwd_map), tok2(DK, bwd_map),
            tok2(DV, both_map), tok2(DV, fwd_map), tok2(D_MODEL, fwd_map),
            tok2(D_MODEL, fwd_map), tok2(D_MODEL, fwd_map),
            d_spec2,
            const3((HEADS, HEAD_K, HEAD_V)), const3((1, HEAD_V)),
            const3((DV, D_MODEL)), const3((D_MODEL, D_MODEL)), const3((1, D_MODEL)),
        ],
        out_specs=tok2(D_MODEL, fwd_map),
        out_shape=jax.ShapeDtypeStruct((bsz, seq, D_MODEL), x.dtype),
        scratch_shapes=[pltpu.VMEM((HEADS, HEAD_K, HEAD_V), F32),
                        pltpu.VMEM((seq, DV), F32),
                        pltpu.VMEM((TILE, DV), F32),
                        pltpu.VMEM((BPT, HEADS, BLOCK, BLOCK), BF16),
                        pltpu.VMEM((BPT, HEADS, HEAD_K, HEAD_V), F32),
                        pltpu.VMEM((BPT, HEADS, HEAD_K, HEAD_V), BF16)],
        compiler_params=pltpu.CompilerParams(
            dimension_semantics=("parallel", "arbitrary", "arbitrary"),
            vmem_limit_bytes=VMEM_LIMIT),
        name="gla_merge_out",
    )(qf, kf, qb, kb, v, rs, smb, a, x, d, s0, glag, wog, wo, gpost)
    return out
```

```python
import jax
import jax.numpy as jnp
from jax import lax
from jax.experimental import pallas as pl
from jax.experimental.pallas import tpu as pltpu

D_MODEL = 1024
N_META = 16
HEADS = 4
DK = 512
DV = 1024
HEAD_K = DK // HEADS
HEAD_V = DV // HEADS
GATE_RANK = 16
GATE_NORMALIZER = 16.0
EPS = 1e-6

COL_CB, COL_CC, COL_CX, COL_CZ = 0, 1024, 2048, 3072
COL_Q, COL_K, COL_V, COL_R = 4096, 4608, 5120, 6144
COL_LR = 7168
COL_MERGE = COL_LR + 2 * GATE_RANK
N_IN = COL_MERGE + 2 * D_MODEL
LR_PAD = 128

HALO = 16
TILE = 512
BLOCK = 128
BPT = TILE // BLOCK
CW = 256
VMEM_LIMIT = 56 * 1024 * 1024

F32 = jnp.float32
BF16 = jnp.bfloat16


def _dot(a, b):
    return jnp.dot(a, b, preferred_element_type=F32)


def _dot_nt(a, b):
    return lax.dot_general(a, b, (((1,), (1,)), ((), ())), preferred_element_type=F32)


def _dot_tn(a, b):
    return lax.dot_general(a, b, (((0,), (0,)), ((), ())), preferred_element_type=F32)


def _rms(x, g):
    ms = jnp.mean(x * x, axis=-1, keepdims=True)
    return x * lax.rsqrt(ms + EPS) * g


def _sigmoid(x):
    return 1.0 / (1.0 + jnp.exp(-x))


def _log_sigmoid(x):
    return jnp.minimum(x, 0.0) - jnp.log1p(jnp.exp(-jnp.abs(x)))


def _sum_matrix(n, ref, reverse):
    r = lax.broadcasted_iota(jnp.int32, (n, 2 * n), 0)
    c = lax.broadcasted_iota(jnp.int32, (n, 2 * n), 1)
    c = jnp.where(c >= n, c - n, c)
    if reverse:
        m = jnp.where(c >= r, 1.0, 0.0) - jnp.where(c >= ref, 1.0, 0.0)
    else:
        m = jnp.where(c <= r, 1.0, 0.0) - jnp.where(c < ref, 1.0, 0.0)
    return m.astype(BF16)


def _row_sums(mat2, g):
    hi = g.astype(BF16)
    lo = (g - hi.astype(F32)).astype(BF16)
    return _dot(mat2, jnp.concatenate([hi, lo], axis=0))


def _meta_kernel(meta_ref, gpre_ref, wcc_ref, wcx_ref, wk_ref, wv_ref, wlr_ref,
                 wg_ref, bg_ref, smeta_ref, s0_ref):
    u = _rms(meta_ref[...], gpre_ref[...]).astype(BF16)
    smeta_ref[...] = _dot(u, wcc_ref[...]) * _dot(u, wcx_ref[...])
    k = _dot(u, wk_ref[...])
    v = _dot(u, wv_ref[...]).astype(BF16)
    lr = _dot(u, wlr_ref[...]).astype(BF16)
    z = _dot(lr, wg_ref[:, :DK]) + bg_ref[:, :DK]
    g = _log_sigmoid(z) * (1.0 / GATE_NORMALIZER)
    b = _row_sums(_sum_matrix(N_META, 0, reverse=False), g)
    kdec = (k * jnp.exp(b[N_META - 1:N_META, :] - b)).astype(BF16)
    for h in range(HEADS):
        s0_ref[h] = _dot_tn(kdec[:, h * HEAD_K:(h + 1) * HEAD_K],
                            v[:, h * HEAD_V:(h + 1) * HEAD_V])


def _proj_kernel(x_ref, xp_ref, xn_ref, smeta_ref, gpre_ref, w_ref, wm_ref, wlr_ref,
                 convw_ref, wg_ref, bg_ref, woc_ref,
                 qf_ref, kf_ref, qb_ref, kb_ref, v_ref, rs_ref, smb_ref, a_ref, d_ref,
                 u_scr, s_scr, y_scr, g_scr, q_scr, k_scr):
    i = pl.program_id(1)
    last = pl.num_programs(1) - 1
    t = TILE
    gpre = gpre_ref[...]

    u_scr[0:HALO, :] = _rms(xp_ref[0], gpre).astype(BF16)
    u_scr[HALO:HALO + t, :] = _rms(x_ref[0], gpre).astype(BF16)
    u_scr[HALO + t:, :] = _rms(xn_ref[0], gpre).astype(BF16)

    def w(col, width=CW):
        return w_ref[:, col:col + width]

    u_main = u_scr[HALO:HALO + t, :]
    lr = _dot(u_main, wlr_ref[...]).astype(BF16)
    g_scr[...] = _log_sigmoid(_dot(lr, wg_ref[...]) + bg_ref[...]) * (1.0 / GATE_NORMALIZER)
    q_scr[...] = _dot(u_main, w(COL_Q, DK)) * (HEAD_K ** -0.5)
    k_scr[...] = _dot(u_main, w(COL_K, DK))

    half = BLOCK // 2
    mat_f = _sum_matrix(BLOCK, half, reverse=False)
    mat_b = _sum_matrix(BLOCK, half, reverse=True)

    def decay_block(c):
        rs = slice(c * BLOCK, (c + 1) * BLOCK)
        qc = q_scr[rs, :]
        kc = k_scr[rs, :]
        gf = g_scr[rs, :DK]
        gb = g_scr[rs, DK:]
        bf = _row_sums(mat_f, gf)
        bb = _row_sums(mat_b, gb)
        qf_ref[0, rs, :] = (qc * jnp.exp(bf)).astype(BF16)
        kf_ref[0, rs, :] = (kc * jnp.exp(-bf)).astype(BF16)
        qb_ref[0, rs, :] = (qc * jnp.exp(bb)).astype(BF16)
        kb_ref[0, rs, :] = (kc * jnp.exp(-bb)).astype(BF16)
        d_ref[0, 0, c:c + 1, :] = jnp.exp(gf[0:1, :] - bf[0:1, :])
        d_ref[0, 0, BPT + c:BPT + c + 1, :] = jnp.exp(bf[BLOCK - 1:BLOCK, :])
        d_ref[0, 0, 2 * BPT + c:2 * BPT + c + 1, :] = jnp.exp(
            gb[BLOCK - 1:BLOCK, :] - bb[BLOCK - 1:BLOCK, :])
        d_ref[0, 0, 3 * BPT + c:3 * BPT + c + 1, :] = jnp.exp(bb[0:1, :])

    assert D_MODEL // CW == BPT
    for j in range(D_MODEL // CW):
        c0 = j * CW
        u_ext = u_scr[...]
        s_scr[...] = _dot(u_ext, w(COL_CC + c0)) * _dot(u_ext, w(COL_CX + c0))
        s_scr[HALO - 1:HALO, :] = jnp.where(
            i == 0, smeta_ref[N_META - 1:N_META, c0:c0 + CW], s_scr[HALO - 1:HALO, :])
        s_scr[HALO + t:HALO + t + 1, :] = jnp.where(
            i == last, 0.0, s_scr[HALO + t:HALO + t + 1, :])
        conv = (s_scr[HALO - 1:HALO - 1 + t, :] * convw_ref[0:1, c0:c0 + CW]
                + s_scr[HALO:HALO + t, :] * convw_ref[1:2, c0:c0 + CW]
                + s_scr[HALO + 1:HALO + 1 + t, :] * convw_ref[2:3, c0:c0 + CW])
        cb = _dot(u_main, w(COL_CB + c0))
        cz = _dot(u_main, w(COL_CZ + c0))
        y_scr[:, c0:c0 + CW] = (cb * conv * (cz * _sigmoid(cz))).astype(BF16)
        decay_block(j)

    for j in range(D_MODEL // CW):
        c0 = j * CW
        pc = _dot(y_scr[...], woc_ref[:, c0:c0 + CW])
        ma = _dot(u_main, wm_ref[:, c0:c0 + CW])
        a_ref[0, :, c0:c0 + CW] = (_sigmoid(ma) * pc).astype(BF16)
        mb = _dot(u_main, wm_ref[:, D_MODEL + c0:D_MODEL + c0 + CW])
        smb_ref[0, :, c0:c0 + CW] = _sigmoid(mb).astype(BF16)
        r = _dot(u_main, w(COL_R + c0))
        rs_ref[0, :, c0:c0 + CW] = (r * _sigmoid(r)).astype(BF16)
        v_ref[0, :, c0:c0 + CW] = _dot(u_main, w(COL_V + c0)).astype(BF16)


def _gla_kernel(qf_ref, kf_ref, qb_ref, kb_ref, v_ref, rs_hbm, smb_hbm, a_hbm, x_hbm,
                d_ref, s0_ref, glag_ref, wog_ref, wo_ref, gpost_ref,
                out_ref, state_scr, ob_scr, o_scr, sc_scr, kv_scr, sb_scr,
                rs_buf, smb_buf, a_buf, x_buf, tail_sem):
    b = pl.program_id(0)
    p = pl.program_id(1)
    i = pl.program_id(2)
    nt = pl.num_programs(2)
    t = TILE

    row = lax.broadcasted_iota(jnp.int32, (BLOCK, BLOCK), 0)
    col = lax.broadcasted_iota(jnp.int32, (BLOCK, BLOCK), 1)
    eye = row == col

    def rows(c):
        return slice(c * BLOCK, (c + 1) * BLOCK)

    def hk(h):
        return slice(h * HEAD_K, (h + 1) * HEAD_K)

    def hv(h):
        return slice(h * HEAD_V, (h + 1) * HEAD_V)

    def as_column(r, h):
        d = d_ref[0, 0, r:r + 1, hk(h)]
        return jnp.sum(jnp.where(eye, d, 0.0), axis=1, keepdims=True)

    def scan_tile(q_ref, k_ref, d_row0, mask, order, emit):
        for c in range(BPT):
            for h in range(HEADS):
                q = q_ref[0, rows(c), hk(h)]
                k = k_ref[0, rows(c), hk(h)]
                sc_scr[c, h] = jnp.where(mask, _dot_nt(q, k), 0.0).astype(BF16)
                kv_scr[c, h] = _dot_tn(k, v_ref[0, rows(c), hv(h)])
        for h in range(HEADS):
            state = state_scr[h]
            for c in order:
                s_in = state * as_column(d_row0 + c, h)
                sb_scr[c, h] = s_in.astype(BF16)
                state = (s_in + kv_scr[c, h]) * as_column(d_row0 + BPT + c, h)
            state_scr[h] = state
        for c in range(BPT):
            for h in range(HEADS):
                lhs = jnp.concatenate([q_ref[0, rows(c), hk(h)], sc_scr[c, h]], axis=1)
                rhs = jnp.concatenate([sb_scr[c, h], v_ref[0, rows(c), hv(h)]], axis=0)
                emit(c, h, _dot(lhs, rhs))

    @pl.when(jnp.logical_and(p == 0, i == 0))
    def _():
        state_scr[...] = jnp.zeros_like(state_scr)

    @pl.when(jnp.logical_and(p == 1, i == 0))
    def _():
        state_scr[...] = s0_ref[...]

    @pl.when(p == 0)
    def _():
        base = (nt - 1 - i) * t

        def emit(c, h, o):
            ob_scr[pl.ds(pl.multiple_of(base + c * BLOCK, BLOCK), BLOCK), hv(h)] = o

        scan_tile(qb_ref, kb_ref, 2 * BPT, col > row, list(reversed(range(BPT))), emit)

    @pl.when(p == 1)
    def _():
        base = i * t

        def tail_copy(k, hbm, buf):
            return pltpu.make_async_copy(
                hbm.at[b, pl.ds(pl.multiple_of(base, TILE), TILE), :], buf, tail_sem.at[k])

        tail_copies = [tail_copy(0, rs_hbm, rs_buf), tail_copy(1, smb_hbm, smb_buf),
                       tail_copy(2, a_hbm, a_buf), tail_copy(3, x_hbm, x_buf)]
        for cp in tail_copies:
            cp.start()

        def emit(c, h, o):
            ob = ob_scr[pl.ds(pl.multiple_of(base + c * BLOCK, BLOCK), BLOCK), hv(h)]
            o_scr[rows(c), hv(h)] = o + ob

        scan_tile(qf_ref, kf_ref, 0, col <= row, list(range(BPT)), emit)
        glag = glag_ref[...]
        for cp in tail_copies:
            cp.wait()
        ys = []
        for h in range(HEADS):
            oh = _rms(o_scr[:, hv(h)], glag)
            ys.append((oh * rs_buf[:, hv(h)].astype(F32)).astype(BF16))
        y = jnp.concatenate(ys, axis=-1)
        p_gla = _dot(y, wog_ref[...])
        merged = a_buf[...].astype(F32) + smb_buf[...].astype(F32) * p_gla
        out = _dot(merged.astype(BF16), wo_ref[...])
        out_ref[0] = x_buf[...] + _rms(out, gpost_ref[...])


def kernel(x, meta_tokens, norm_pre, w_in, conv_w, w_gate_fwd, b_gate_fwd, w_gate_bwd,
           b_gate_bwd, gla_norm, w_out_conv, w_out_gla, w_merge_out, norm_post):
    bsz, seq, _ = x.shape
    assert seq % TILE == 0 and norm_pre.shape[0] == 1
    nt = seq // TILE

    w_main = w_in[0].astype(BF16)
    w_merge = w_main[:, COL_MERGE:]
    w_lr = jnp.pad(w_main[:, COL_LR:COL_MERGE], ((0, 0), (0, LR_PAD - 2 * GATE_RANK)))
    wg = jnp.zeros((LR_PAD, 2 * DK), F32)
    wg = wg.at[:GATE_RANK, :DK].set(w_gate_fwd[0])
    wg = wg.at[GATE_RANK:2 * GATE_RANK, DK:].set(w_gate_bwd[0]).astype(BF16)
    bg = jnp.concatenate([b_gate_fwd[0], b_gate_bwd[0]])[None, :]
    woc = w_out_conv[0].astype(BF16)
    wog = w_out_gla[0].astype(BF16)
    wo = w_merge_out[0].astype(BF16)
    gpre = norm_pre[0][None, :]
    gpost = norm_post[0][None, :]
    glag = gla_norm[0][None, :]
    convw = conv_w[0]

    def col_spec(width, col):
        return pl.BlockSpec((D_MODEL, width), lambda g, c=col // width: (0, c))

    def full1(shape):
        return pl.BlockSpec(shape, lambda g: (0,) * len(shape))

    smeta, s0 = pl.pallas_call(
        _meta_kernel,
        grid=(1,),
        in_specs=[full1((N_META, D_MODEL)), full1((1, D_MODEL)),
                  col_spec(1024, COL_CC), col_spec(1024, COL_CX),
                  col_spec(DK, COL_K), col_spec(DV, COL_V), full1((D_MODEL, LR_PAD)),
                  full1((LR_PAD, 2 * DK)), full1((1, 2 * DK))],
        out_specs=[full1((N_META, D_MODEL)), full1((HEADS, HEAD_K, HEAD_V))],
        out_shape=[jax.ShapeDtypeStruct((N_META, D_MODEL), F32),
                   jax.ShapeDtypeStruct((HEADS, HEAD_K, HEAD_V), F32)],
        compiler_params=pltpu.CompilerParams(vmem_limit_bytes=VMEM_LIMIT),
        name="meta_prologue",
    )(meta_tokens, gpre, w_main, w_main, w_main, w_main, w_lr, wg, bg)

    hb = TILE // HALO
    n_hb = seq // HALO

    def const2(shape):
        return pl.BlockSpec(shape, lambda b, i: (0,) * len(shape),
                            pipeline_mode=pl.Buffered(1))

    def tok_spec(width):
        return pl.BlockSpec((1, TILE, width), lambda b, i: (b, i, 0))

    d_spec = pl.BlockSpec((1, 1, 4 * BPT, DK), lambda b, i: (b, i, 0, 0))
    tok_shape = lambda width: jax.ShapeDtypeStruct((bsz, seq, width), BF16)
    d_shape = jax.ShapeDtypeStruct((bsz, nt, 4 * BPT, DK), F32)

    qf, kf, qb, kb, v, rs, smb, a, d = pl.pallas_call(
        _proj_kernel,
        grid=(bsz, nt),
        in_specs=[
            tok_spec(D_MODEL),
            pl.BlockSpec((1, HALO, D_MODEL),
                         lambda b, i: (b, jnp.maximum(i * hb - 1, 0), 0)),
            pl.BlockSpec((1, HALO, D_MODEL),
                         lambda b, i: (b, jnp.minimum((i + 1) * hb, n_hb - 1), 0)),
            const2((N_META, D_MODEL)), const2((1, D_MODEL)),
            const2((D_MODEL, N_IN)), const2((D_MODEL, 2 * D_MODEL)),
            const2((D_MODEL, LR_PAD)),
            const2((3, D_MODEL)), const2((LR_PAD, 2 * DK)), const2((1, 2 * DK)),
            const2((D_MODEL, D_MODEL)),
        ],
        out_specs=[tok_spec(DK), tok_spec(DK), tok_spec(DK), tok_spec(DK),
                   tok_spec(DV), tok_spec(DV), tok_spec(D_MODEL), tok_spec(D_MODEL),
                   d_spec],
        out_shape=[tok_shape(DK), tok_shape(DK), tok_shape(DK), tok_shape(DK),
                   tok_shape(DV), tok_shape(DV), tok_shape(D_MODEL), tok_shape(D_MODEL),
                   d_shape],
        scratch_shapes=[pltpu.VMEM((TILE + 2 * HALO, D_MODEL), BF16),
                        pltpu.VMEM((TILE + 2 * HALO, CW), F32),
                        pltpu.VMEM((TILE, D_MODEL), BF16),
                        pltpu.VMEM((TILE, 2 * DK), F32),
                        pltpu.VMEM((TILE, DK), F32),
                        pltpu.VMEM((TILE, DK), F32)],
        compiler_params=pltpu.CompilerParams(
            dimension_semantics=("parallel", "arbitrary"),
            vmem_limit_bytes=VMEM_LIMIT),
        name="inproj_conv_gates",
    )(x, x, x, smeta, gpre, w_main, w_merge, w_lr, convw, wg, bg, woc)

    def fwd_map(b, p, i):
        return (b, p * i, 0)

    def bwd_map(b, p, i):
        return (b, (1 - p) * (nt - 1 - i), 0)

    def both_map(b, p, i):
        return (b, p * i + (1 - p) * (nt - 1 - i), 0)

    def tok2(width, imap):
        return pl.BlockSpec((1, TILE, width), imap)

    def const3(shape):
        return pl.BlockSpec(shape, lambda b, p, i: (0,) * len(shape))

    d_spec2 = pl.BlockSpec((1, 1, 4 * BPT, DK),
                           lambda b, p, i: (b, p * i + (1 - p) * (nt - 1 - i), 0, 0))
    hbm_spec = pl.BlockSpec(memory_space=pl.ANY)

    out = pl.pallas_call(
        _gla_kernel,
        grid=(bsz, 2, nt),
        in_specs=[
            tok2(DK, fwd_map), tok2(DK, fwd_map), tok2(DK, bwd_map), tok2(DK, bwd_map),
            tok2(DV, both_map), hbm_spec, hbm_spec, hbm_spec, hbm_spec,
            d_spec2,
            const3((HEADS, HEAD_K, HEAD_V)), const3((1, HEAD_V)),
            const3((DV, D_MODEL)), const3((D_MODEL, D_MODEL)), const3((1, D_MODEL)),
        ],
        out_specs=tok2(D_MODEL, fwd_map),
        out_shape=jax.ShapeDtypeStruct((bsz, seq, D_MODEL), x.dtype),
        scratch_shapes=[pltpu.VMEM((HEADS, HEAD_K, HEAD_V), F32),
                        pltpu.VMEM((seq, DV), F32),
                        pltpu.VMEM((TILE, DV), F32),
                        pltpu.VMEM((BPT, HEADS, BLOCK, BLOCK), BF16),
                        pltpu.VMEM((BPT, HEADS, HEAD_K, HEAD_V), F32),
                        pltpu.VMEM((BPT, HEADS, HEAD_K, HEAD_V), BF16),
                        pltpu.VMEM((TILE, DV), BF16),
                        pltpu.VMEM((TILE, D_MODEL), BF16),
                        pltpu.VMEM((TILE, D_MODEL), BF16),
                        pltpu.VMEM((TILE, D_MODEL), F32),
                        pltpu.SemaphoreType.DMA((4,))],
        compiler_params=pltpu.CompilerParams(
            dimension_semantics=("arbitrary", "arbitrary", "arbitrary"),
            vmem_limit_bytes=VMEM_LIMIT),
        name="gla_merge_out",
    )(qf, kf, qb, kb, v, rs, smb, a, x, d, s0, glag, wog, wo, gpost)
    return out
```

```python
import jax
import jax.numpy as jnp
from jax import lax
from jax.experimental import pallas as pl
from jax.experimental.pallas import tpu as pltpu

D_MODEL = 1024
N_META = 16
HEADS = 4
DK = 512
DV = 1024
HEAD_K = DK // HEADS
HEAD_V = DV // HEADS
GATE_RANK = 16
GATE_NORMALIZER = 16.0
EPS = 1e-6

COL_CB, COL_CC, COL_CX, COL_CZ = 0, 1024, 2048, 3072
COL_Q, COL_K, COL_V, COL_R = 4096, 4608, 5120, 6144
COL_LR = 7168
COL_MERGE = COL_LR + 2 * GATE_RANK
N_IN = COL_MERGE + 2 * D_MODEL
LR_PAD = 128

HALO = 16
TILE = 512
BLOCK = 128
BPT = TILE // BLOCK
CW = 256
VMEM_LIMIT = 56 * 1024 * 1024
X_SLOTS = 3

F32 = jnp.float32
BF16 = jnp.bfloat16


def _dot(a, b):
    return jnp.dot(a, b, preferred_element_type=F32)


def _dot_nt(a, b):
    return lax.dot_general(a, b, (((1,), (1,)), ((), ())), preferred_element_type=F32)


def _dot_tn(a, b):
    return lax.dot_general(a, b, (((0,), (0,)), ((), ())), preferred_element_type=F32)


def _rms(x, g):
    ms = jnp.mean(x * x, axis=-1, keepdims=True)
    return x * lax.rsqrt(ms + EPS) * g


def _sigmoid(x):
    return 1.0 / (1.0 + jnp.exp(-x))


def _log_sigmoid(x):
    return jnp.minimum(x, 0.0) - jnp.log1p(jnp.exp(-jnp.abs(x)))


def _sum_matrix(n, ref, reverse):
    r = lax.broadcasted_iota(jnp.int32, (n, 2 * n), 0)
    c = lax.broadcasted_iota(jnp.int32, (n, 2 * n), 1)
    c = jnp.where(c >= n, c - n, c)
    if reverse:
        m = jnp.where(c >= r, 1.0, 0.0) - jnp.where(c >= ref, 1.0, 0.0)
    else:
        m = jnp.where(c <= r, 1.0, 0.0) - jnp.where(c < ref, 1.0, 0.0)
    return m.astype(BF16)


def _row_sums(mat2, g):
    hi = g.astype(BF16)
    lo = (g - hi.astype(F32)).astype(BF16)
    return _dot(mat2, jnp.concatenate([hi, lo], axis=0))


def _meta_kernel(meta_ref, gpre_ref, wcc_ref, wcx_ref, wk_ref, wv_ref, wlr_ref,
                 wg_ref, bg_ref, smeta_ref, s0_ref):
    u = _rms(meta_ref[...], gpre_ref[...]).astype(BF16)
    smeta_ref[...] = _dot(u, wcc_ref[...]) * _dot(u, wcx_ref[...])
    k = _dot(u, wk_ref[...])
    v = _dot(u, wv_ref[...]).astype(BF16)
    lr = _dot(u, wlr_ref[...]).astype(BF16)
    z = _dot(lr, wg_ref[:, :DK]) + bg_ref[:, :DK]
    g = _log_sigmoid(z) * (1.0 / GATE_NORMALIZER)
    b = _row_sums(_sum_matrix(N_META, 0, reverse=False), g)
    kdec = (k * jnp.exp(b[N_META - 1:N_META, :] - b)).astype(BF16)
    for h in range(HEADS):
        s0_ref[h] = _dot_tn(kdec[:, h * HEAD_K:(h + 1) * HEAD_K],
                            v[:, h * HEAD_V:(h + 1) * HEAD_V])


def _proj_kernel(x_ref, xp_ref, xn_ref, smeta_ref, gpre_ref, w_ref, wm_ref, wlr_ref,
                 convw_ref, wg_ref, bg_ref, woc_ref,
                 qf_ref, kf_ref, qb_ref, kb_ref, v_ref, rs_ref, smb_ref, a_ref, d_ref,
                 u_scr, s_scr, y_scr, g_scr, q_scr, k_scr):
    i = pl.program_id(1)
    last = pl.num_programs(1) - 1
    t = TILE
    gpre = gpre_ref[...]

    u_scr[0:HALO, :] = _rms(xp_ref[0], gpre).astype(BF16)
    u_scr[HALO:HALO + t, :] = _rms(x_ref[0], gpre).astype(BF16)
    u_scr[HALO + t:, :] = _rms(xn_ref[0], gpre).astype(BF16)

    def w(col, width=CW):
        return w_ref[:, col:col + width]

    u_main = u_scr[HALO:HALO + t, :]
    lr = _dot(u_main, wlr_ref[...]).astype(BF16)
    g_scr[...] = _log_sigmoid(_dot(lr, wg_ref[...]) + bg_ref[...]) * (1.0 / GATE_NORMALIZER)
    q_scr[...] = _dot(u_main, w(COL_Q, DK)) * (HEAD_K ** -0.5)
    k_scr[...] = _dot(u_main, w(COL_K, DK))

    half = BLOCK // 2
    mat_f = _sum_matrix(BLOCK, half, reverse=False)
    mat_b = _sum_matrix(BLOCK, half, reverse=True)

    def decay_block(c):
        rs = slice(c * BLOCK, (c + 1) * BLOCK)
        qc = q_scr[rs, :]
        kc = k_scr[rs, :]
        gf = g_scr[rs, :DK]
        gb = g_scr[rs, DK:]
        bf = _row_sums(mat_f, gf)
        bb = _row_sums(mat_b, gb)
        qf_ref[0, rs, :] = (qc * jnp.exp(bf)).astype(BF16)
        kf_ref[0, rs, :] = (kc * jnp.exp(-bf)).astype(BF16)
        qb_ref[0, rs, :] = (qc * jnp.exp(bb)).astype(BF16)
        kb_ref[0, rs, :] = (kc * jnp.exp(-bb)).astype(BF16)
        d_ref[0, 0, c:c + 1, :] = jnp.exp(gf[0:1, :] - bf[0:1, :])
        d_ref[0, 0, BPT + c:BPT + c + 1, :] = jnp.exp(bf[BLOCK - 1:BLOCK, :])
        d_ref[0, 0, 2 * BPT + c:2 * BPT + c + 1, :] = jnp.exp(
            gb[BLOCK - 1:BLOCK, :] - bb[BLOCK - 1:BLOCK, :])
        d_ref[0, 0, 3 * BPT + c:3 * BPT + c + 1, :] = jnp.exp(bb[0:1, :])

    assert D_MODEL // CW == BPT
    for j in range(D_MODEL // CW):
        c0 = j * CW
        u_ext = u_scr[...]
        s_scr[...] = _dot(u_ext, w(COL_CC + c0)) * _dot(u_ext, w(COL_CX + c0))
        s_scr[HALO - 1:HALO, :] = jnp.where(
            i == 0, smeta_ref[N_META - 1:N_META, c0:c0 + CW], s_scr[HALO - 1:HALO, :])
        s_scr[HALO + t:HALO + t + 1, :] = jnp.where(
            i == last, 0.0, s_scr[HALO + t:HALO + t + 1, :])
        conv = (s_scr[HALO - 1:HALO - 1 + t, :] * convw_ref[0:1, c0:c0 + CW]
                + s_scr[HALO:HALO + t, :] * convw_ref[1:2, c0:c0 + CW]
                + s_scr[HALO + 1:HALO + 1 + t, :] * convw_ref[2:3, c0:c0 + CW])
        cb = _dot(u_main, w(COL_CB + c0))
        cz = _dot(u_main, w(COL_CZ + c0))
        y_scr[:, c0:c0 + CW] = (cb * conv * (cz * _sigmoid(cz))).astype(BF16)
        decay_block(j)

    for j in range(D_MODEL // CW):
        c0 = j * CW
        pc = _dot(y_scr[...], woc_ref[:, c0:c0 + CW])
        ma = _dot(u_main, wm_ref[:, c0:c0 + CW])
        a_ref[0, :, c0:c0 + CW] = (_sigmoid(ma) * pc).astype(BF16)
        mb = _dot(u_main, wm_ref[:, D_MODEL + c0:D_MODEL + c0 + CW])
        smb_ref[0, :, c0:c0 + CW] = _sigmoid(mb).astype(BF16)
        r = _dot(u_main, w(COL_R + c0))
        rs_ref[0, :, c0:c0 + CW] = (r * _sigmoid(r)).astype(BF16)
        v_ref[0, :, c0:c0 + CW] = _dot(u_main, w(COL_V + c0)).astype(BF16)


def _gla_kernel(qf_ref, kf_ref, qb_ref, kb_ref, v_ref, rs_ref, smb_ref, a_ref, x_hbm,
                d_ref, s0_ref, glag_ref, wog_ref, wo_ref, gpost_ref,
                out_ref, state_scr, ob_scr, o_scr, sc_scr, kv_scr, sb_scr, x_ring, x_sem):
    b = pl.program_id(0)
    p = pl.program_id(1)
    i = pl.program_id(2)
    nt = pl.num_programs(2)
    t = TILE

    def x_copy(tile):
        slot = tile % X_SLOTS
        return pltpu.make_async_copy(
            x_hbm.at[b, pl.ds(pl.multiple_of(tile * TILE, TILE), TILE), :],
            x_ring.at[slot], x_sem.at[slot])

    row = lax.broadcasted_iota(jnp.int32, (BLOCK, BLOCK), 0)
    col = lax.broadcasted_iota(jnp.int32, (BLOCK, BLOCK), 1)
    eye = row == col

    def rows(c):
        return slice(c * BLOCK, (c + 1) * BLOCK)

    def hk(h):
        return slice(h * HEAD_K, (h + 1) * HEAD_K)

    def hv(h):
        return slice(h * HEAD_V, (h + 1) * HEAD_V)

    def as_column(r, h):
        d = d_ref[0, 0, r:r + 1, hk(h)]
        return jnp.sum(jnp.where(eye, d, 0.0), axis=1, keepdims=True)

    def scan_tile(q_ref, k_ref, d_row0, mask, order, emit):
        for c in range(BPT):
            for h in range(HEADS):
                q = q_ref[0, rows(c), hk(h)]
                k = k_ref[0, rows(c), hk(h)]
                sc_scr[c, h] = jnp.where(mask, _dot_nt(q, k), 0.0).astype(BF16)
                kv_scr[c, h] = _dot_tn(k, v_ref[0, rows(c), hv(h)])
        for h in range(HEADS):
            state = state_scr[h]
            for c in order:
                s_in = state * as_column(d_row0 + c, h)
                sb_scr[c, h] = s_in.astype(BF16)
                state = (s_in + kv_scr[c, h]) * as_column(d_row0 + BPT + c, h)
            state_scr[h] = state
        for c in range(BPT):
            for h in range(HEADS):
                lhs = jnp.concatenate([q_ref[0, rows(c), hk(h)], sc_scr[c, h]], axis=1)
                rhs = jnp.concatenate([sb_scr[c, h], v_ref[0, rows(c), hv(h)]], axis=0)
                emit(c, h, _dot(lhs, rhs))

    @pl.when(jnp.logical_and(p == 0, i == 0))
    def _():
        state_scr[...] = jnp.zeros_like(state_scr)

    @pl.when(jnp.logical_and(p == 1, i == 0))
    def _():
        state_scr[...] = s0_ref[...]

    @pl.when(p == 0)
    def _():
        base = (nt - 1 - i) * t

        @pl.when(i >= nt - 2)
        def _():
            x_copy(i - (nt - 2)).start()

        def emit(c, h, o):
            ob_scr[pl.ds(pl.multiple_of(base + c * BLOCK, BLOCK), BLOCK), hv(h)] = o

        scan_tile(qb_ref, kb_ref, 2 * BPT, col > row, list(reversed(range(BPT))), emit)

    @pl.when(p == 1)
    def _():
        base = i * t

        @pl.when(i + 2 < nt)
        def _():
            x_copy(i + 2).start()

        x_copy(i).wait()

        def emit(c, h, o):
            ob = ob_scr[pl.ds(pl.multiple_of(base + c * BLOCK, BLOCK), BLOCK), hv(h)]
            o_scr[rows(c), hv(h)] = o + ob

        scan_tile(qf_ref, kf_ref, 0, col <= row, list(range(BPT)), emit)
        glag = glag_ref[...]
        ys = []
        for h in range(HEADS):
            oh = _rms(o_scr[:, hv(h)], glag)
            ys.append((oh * rs_ref[0, :, hv(h)].astype(F32)).astype(BF16))
        y = jnp.concatenate(ys, axis=-1)
        p_gla = _dot(y, wog_ref[...])
        merged = a_ref[0].astype(F32) + smb_ref[0].astype(F32) * p_gla
        out = _dot(merged.astype(BF16), wo_ref[...])
        out_ref[0] = x_ring[i % X_SLOTS] + _rms(out, gpost_ref[...])


def kernel(x, meta_tokens, norm_pre, w_in, conv_w, w_gate_fwd, b_gate_fwd, w_gate_bwd,
           b_gate_bwd, gla_norm, w_out_conv, w_out_gla, w_merge_out, norm_post):
    bsz, seq, _ = x.shape
    assert seq % TILE == 0 and norm_pre.shape[0] == 1
    nt = seq // TILE
    assert 2 <= nt

    w_main = w_in[0].astype(BF16)
    w_merge = w_main[:, COL_MERGE:]
    w_lr = jnp.pad(w_main[:, COL_LR:COL_MERGE], ((0, 0), (0, LR_PAD - 2 * GATE_RANK)))
    wg = jnp.zeros((LR_PAD, 2 * DK), F32)
    wg = wg.at[:GATE_RANK, :DK].set(w_gate_fwd[0])
    wg = wg.at[GATE_RANK:2 * GATE_RANK, DK:].set(w_gate_bwd[0]).astype(BF16)
    bg = jnp.concatenate([b_gate_fwd[0], b_gate_bwd[0]])[None, :]
    woc = w_out_conv[0].astype(BF16)
    wog = w_out_gla[0].astype(BF16)
    wo = w_merge_out[0].astype(BF16)
    gpre = norm_pre[0][None, :]
    gpost = norm_post[0][None, :]
    glag = gla_norm[0][None, :]
    convw = conv_w[0]

    def col_spec(width, col):
        return pl.BlockSpec((D_MODEL, width), lambda g, c=col // width: (0, c))

    def full1(shape):
        return pl.BlockSpec(shape, lambda g: (0,) * len(shape))

    smeta, s0 = pl.pallas_call(
        _meta_kernel,
        grid=(1,),
        in_specs=[full1((N_META, D_MODEL)), full1((1, D_MODEL)),
                  col_spec(1024, COL_CC), col_spec(1024, COL_CX),
                  col_spec(DK, COL_K), col_spec(DV, COL_V), full1((D_MODEL, LR_PAD)),
                  full1((LR_PAD, 2 * DK)), full1((1, 2 * DK))],
        out_specs=[full1((N_META, D_MODEL)), full1((HEADS, HEAD_K, HEAD_V))],
        out_shape=[jax.ShapeDtypeStruct((N_META, D_MODEL), F32),
                   jax.ShapeDtypeStruct((HEADS, HEAD_K, HEAD_V), F32)],
        compiler_params=pltpu.CompilerParams(vmem_limit_bytes=VMEM_LIMIT),
        name="meta_prologue",
    )(meta_tokens, gpre, w_main, w_main, w_main, w_main, w_lr, wg, bg)

    hb = TILE // HALO
    n_hb = seq // HALO

    def const2(shape):
        return pl.BlockSpec(shape, lambda b, i: (0,) * len(shape),
                            pipeline_mode=pl.Buffered(1))

    def tok_spec(width):
        return pl.BlockSpec((1, TILE, width), lambda b, i: (b, i, 0))

    d_spec = pl.BlockSpec((1, 1, 4 * BPT, DK), lambda b, i: (b, i, 0, 0))
    tok_shape = lambda width: jax.ShapeDtypeStruct((bsz, seq, width), BF16)
    d_shape = jax.ShapeDtypeStruct((bsz, nt, 4 * BPT, DK), F32)

    qf, kf, qb, kb, v, rs, smb, a, d = pl.pallas_call(
        _proj_kernel,
        grid=(bsz, nt),
        in_specs=[
            tok_spec(D_MODEL),
            pl.BlockSpec((1, HALO, D_MODEL),
                         lambda b, i: (b, jnp.maximum(i * hb - 1, 0), 0)),
            pl.BlockSpec((1, HALO, D_MODEL),
                         lambda b, i: (b, jnp.minimum((i + 1) * hb, n_hb - 1), 0)),
            const2((N_META, D_MODEL)), const2((1, D_MODEL)),
            const2((D_MODEL, N_IN)), const2((D_MODEL, 2 * D_MODEL)),
            const2((D_MODEL, LR_PAD)),
            const2((3, D_MODEL)), const2((LR_PAD, 2 * DK)), const2((1, 2 * DK)),
            const2((D_MODEL, D_MODEL)),
        ],
        out_specs=[tok_spec(DK), tok_spec(DK), tok_spec(DK), tok_spec(DK),
                   tok_spec(DV), tok_spec(DV), tok_spec(D_MODEL), tok_spec(D_MODEL),
                   d_spec],
        out_shape=[tok_shape(DK), tok_shape(DK), tok_shape(DK), tok_shape(DK),
                   tok_shape(DV), tok_shape(DV), tok_shape(D_MODEL), tok_shape(D_MODEL),
                   d_shape],
        scratch_shapes=[pltpu.VMEM((TILE + 2 * HALO, D_MODEL), BF16),
                        pltpu.VMEM((TILE + 2 * HALO, CW), F32),
                        pltpu.VMEM((TILE, D_MODEL), BF16),
                        pltpu.VMEM((TILE, 2 * DK), F32),
                        pltpu.VMEM((TILE, DK), F32),
                        pltpu.VMEM((TILE, DK), F32)],
        compiler_params=pltpu.CompilerParams(
            dimension_semantics=("parallel", "arbitrary"),
            vmem_limit_bytes=VMEM_LIMIT),
        name="inproj_conv_gates",
    )(x, x, x, smeta, gpre, w_main, w_merge, w_lr, convw, wg, bg, woc)

    def fwd_map(b, p, i):
        return (b, p * i, 0)

    def bwd_map(b, p, i):
        return (b, (1 - p) * (nt - 1 - i), 0)

    def both_map(b, p, i):
        return (b, p * i + (1 - p) * (nt - 1 - i), 0)

    def tok2(width, imap):
        return pl.BlockSpec((1, TILE, width), imap)

    def const3(shape):
        return pl.BlockSpec(shape, lambda b, p, i: (0,) * len(shape))

    d_spec2 = pl.BlockSpec((1, 1, 4 * BPT, DK),
                           lambda b, p, i: (b, p * i + (1 - p) * (nt - 1 - i), 0, 0))

    out = pl.pallas_call(
        _gla_kernel,
        grid=(bsz, 2, nt),
        in_specs=[
            tok2(DK, fwd_map), tok2(DK, fwd_map), tok2(DK, bwd_map), tok2(DK, bwd_map),
            tok2(DV, both_map), tok2(DV, fwd_map), tok2(D_MODEL, fwd_map),
            tok2(D_MODEL, fwd_map), pl.BlockSpec(memory_space=pl.ANY),
            d_spec2,
            const3((HEADS, HEAD_K, HEAD_V)), const3((1, HEAD_V)),
            const3((DV, D_MODEL)), const3((D_MODEL, D_MODEL)), const3((1, D_MODEL)),
        ],
        out_specs=tok2(D_MODEL, fwd_map),
        out_shape=jax.ShapeDtypeStruct((bsz, seq, D_MODEL), x.dtype),
        scratch_shapes=[pltpu.VMEM((HEADS, HEAD_K, HEAD_V), F32),
                        pltpu.VMEM((seq, DV), F32),
                        pltpu.VMEM((TILE, DV), F32),
                        pltpu.VMEM((BPT, HEADS, BLOCK, BLOCK), BF16),
                        pltpu.VMEM((BPT, HEADS, HEAD_K, HEAD_V), F32),
                        pltpu.VMEM((BPT, HEADS, HEAD_K, HEAD_V), BF16),
                        pltpu.VMEM((X_SLOTS, TILE, D_MODEL), F32),
                        pltpu.SemaphoreType.DMA((X_SLOTS,))],
        compiler_params=pltpu.CompilerParams(
            dimension_semantics=("arbitrary", "arbitrary", "arbitrary"),
            vmem_limit_bytes=VMEM_LIMIT),
        name="gla_merge_out",
    )(qf, kf, qb, kb, v, rs, smb, a, x, d, s0, glag, wog, wo, gpost)
    return out
```

```python
import jax
import jax.numpy as jnp
from jax import lax
from jax.experimental import pallas as pl
from jax.experimental.pallas import tpu as pltpu

D_MODEL = 1024
N_META = 16
HEADS = 4
DK = 512
DV = 1024
HEAD_K = DK // HEADS
HEAD_V = DV // HEADS
GATE_RANK = 16
GATE_NORMALIZER = 16.0
EPS = 1e-6

COL_CB, COL_CC, COL_CX, COL_CZ = 0, 1024, 2048, 3072
COL_Q, COL_K, COL_V, COL_R = 4096, 4608, 5120, 6144
COL_LR = 7168
COL_MERGE = COL_LR + 2 * GATE_RANK
N_IN = COL_MERGE + 2 * D_MODEL
LR_PAD = 128

HALO = 16
TILE = 512
BLOCK = 128
BPT = TILE // BLOCK
CW = 256
VMEM_LIMIT = 56 * 1024 * 1024
X_SLOTS = 3

F32 = jnp.float32
BF16 = jnp.bfloat16


def _dot(a, b):
    return jnp.dot(a, b, preferred_element_type=F32)


def _dot_nt(a, b):
    return lax.dot_general(a, b, (((1,), (1,)), ((), ())), preferred_element_type=F32)


def _dot_tn(a, b):
    return lax.dot_general(a, b, (((0,), (0,)), ((), ())), preferred_element_type=F32)


def _rms(x, g):
    ms = jnp.mean(x * x, axis=-1, keepdims=True)
    return x * lax.rsqrt(ms + EPS) * g


def _sigmoid(x):
    return 1.0 / (1.0 + jnp.exp(-x))


def _log_sigmoid(x):
    return jnp.minimum(x, 0.0) - jnp.log1p(jnp.exp(-jnp.abs(x)))


def _sum_matrix(n, ref, reverse):
    r = lax.broadcasted_iota(jnp.int32, (n, 2 * n), 0)
    c = lax.broadcasted_iota(jnp.int32, (n, 2 * n), 1)
    c = jnp.where(c >= n, c - n, c)
    if reverse:
        m = jnp.where(c >= r, 1.0, 0.0) - jnp.where(c >= ref, 1.0, 0.0)
    else:
        m = jnp.where(c <= r, 1.0, 0.0) - jnp.where(c < ref, 1.0, 0.0)
    return m.astype(BF16)


def _row_sums(mat2, g):
    hi = g.astype(BF16)
    lo = (g - hi.astype(F32)).astype(BF16)
    return _dot(mat2, jnp.concatenate([hi, lo], axis=0))


def _meta_kernel(meta_ref, gpre_ref, wcc_ref, wcx_ref, wk_ref, wv_ref, wlr_ref,
                 wg_ref, bg_ref, smeta_ref, s0_ref):
    u = _rms(meta_ref[...], gpre_ref[...]).astype(BF16)
    smeta_ref[...] = _dot(u, wcc_ref[...]) * _dot(u, wcx_ref[...])
    k = _dot(u, wk_ref[...])
    v = _dot(u, wv_ref[...]).astype(BF16)
    lr = _dot(u, wlr_ref[...]).astype(BF16)
    z = _dot(lr, wg_ref[:, :DK]) + bg_ref[:, :DK]
    g = _log_sigmoid(z) * (1.0 / GATE_NORMALIZER)
    b = _row_sums(_sum_matrix(N_META, 0, reverse=False), g)
    kdec = (k * jnp.exp(b[N_META - 1:N_META, :] - b)).astype(BF16)
    for h in range(HEADS):
        s0_ref[h] = _dot_tn(kdec[:, h * HEAD_K:(h + 1) * HEAD_K],
                            v[:, h * HEAD_V:(h + 1) * HEAD_V])


def _proj_kernel(x_ref, xp_ref, xn_ref, smeta_ref, gpre_ref, w_ref, wm_ref, wlr_ref,
                 convw_ref, wg_ref, bg_ref, woc_ref,
                 qf_ref, kf_ref, qb_ref, kb_ref, v_ref, rs_ref, smb_ref, a_ref, d_ref,
                 u_scr, s_scr, y_scr, g_scr, q_scr, k_scr):
    i = pl.program_id(1)
    last = pl.num_programs(1) - 1
    t = TILE
    gpre = gpre_ref[...]

    u_scr[0:HALO, :] = _rms(xp_ref[0], gpre).astype(BF16)
    u_scr[HALO:HALO + t, :] = _rms(x_ref[0], gpre).astype(BF16)
    u_scr[HALO + t:, :] = _rms(xn_ref[0], gpre).astype(BF16)

    def w(col, width=CW):
        return w_ref[:, col:col + width]

    u_main = u_scr[HALO:HALO + t, :]
    lr = _dot(u_main, wlr_ref[...]).astype(BF16)
    g_scr[...] = _log_sigmoid(_dot(lr, wg_ref[...]) + bg_ref[...]) * (1.0 / GATE_NORMALIZER)
    q_scr[...] = _dot(u_main, w(COL_Q, DK)) * (HEAD_K ** -0.5)
    k_scr[...] = _dot(u_main, w(COL_K, DK))

    half = BLOCK // 2
    mat_f = _sum_matrix(BLOCK, half, reverse=False)
    mat_b = _sum_matrix(BLOCK, half, reverse=True)

    def decay_block(c):
        rs = slice(c * BLOCK, (c + 1) * BLOCK)
        qc = q_scr[rs, :]
        kc = k_scr[rs, :]
        gf = g_scr[rs, :DK]
        gb = g_scr[rs, DK:]
        bf = _row_sums(mat_f, gf)
        bb = _row_sums(mat_b, gb)
        qf_ref[0, rs, :] = (qc * jnp.exp(bf)).astype(BF16)
        kf_ref[0, rs, :] = (kc * jnp.exp(-bf)).astype(BF16)
        qb_ref[0, rs, :] = (qc * jnp.exp(bb)).astype(BF16)
        kb_ref[0, rs, :] = (kc * jnp.exp(-bb)).astype(BF16)
        d_ref[0, 0, c:c + 1, :] = jnp.exp(gf[0:1, :] - bf[0:1, :])
        d_ref[0, 0, BPT + c:BPT + c + 1, :] = jnp.exp(bf[BLOCK - 1:BLOCK, :])
        d_ref[0, 0, 2 * BPT + c:2 * BPT + c + 1, :] = jnp.exp(
            gb[BLOCK - 1:BLOCK, :] - bb[BLOCK - 1:BLOCK, :])
        d_ref[0, 0, 3 * BPT + c:3 * BPT + c + 1, :] = jnp.exp(bb[0:1, :])

    assert D_MODEL // CW == BPT
    for j in range(D_MODEL // CW):
        c0 = j * CW
        u_ext = u_scr[...]
        s_scr[...] = _dot(u_ext, w(COL_CC + c0)) * _dot(u_ext, w(COL_CX + c0))
        s_scr[HALO - 1:HALO, :] = jnp.where(
            i == 0, smeta_ref[N_META - 1:N_META, c0:c0 + CW], s_scr[HALO - 1:HALO, :])
        s_scr[HALO + t:HALO + t + 1, :] = jnp.where(
            i == last, 0.0, s_scr[HALO + t:HALO + t + 1, :])
        conv = (s_scr[HALO - 1:HALO - 1 + t, :] * convw_ref[0:1, c0:c0 + CW]
                + s_scr[HALO:HALO + t, :] * convw_ref[1:2, c0:c0 + CW]
                + s_scr[HALO + 1:HALO + 1 + t, :] * convw_ref[2:3, c0:c0 + CW])
        cb = _dot(u_main, w(COL_CB + c0))
        cz = _dot(u_main, w(COL_CZ + c0))
        y_scr[:, c0:c0 + CW] = (cb * conv * (cz * _sigmoid(cz))).astype(BF16)
        decay_block(j)

    for j in range(D_MODEL // CW):
        c0 = j * CW
        pc = _dot(y_scr[...], woc_ref[:, c0:c0 + CW])
        ma = _dot(u_main, wm_ref[:, c0:c0 + CW])
        a_ref[0, :, c0:c0 + CW] = (_sigmoid(ma) * pc).astype(BF16)
        mb = _dot(u_main, wm_ref[:, D_MODEL + c0:D_MODEL + c0 + CW])
        smb_ref[0, :, c0:c0 + CW] = _sigmoid(mb).astype(BF16)
        r = _dot(u_main, w(COL_R + c0))
        rs_ref[0, :, c0:c0 + CW] = (r * _sigmoid(r)).astype(BF16)
        v_ref[0, :, c0:c0 + CW] = _dot(u_main, w(COL_V + c0)).astype(BF16)


def _gla_kernel(qf_ref, kf_ref, qb_ref, kb_ref, v_ref, rs_hbm, smb_hbm, a_hbm, x_hbm,
                d_ref, s0_ref, glag_ref, wog_ref, wo_ref, gpost_ref,
                out_ref, state_scr, ob_scr, o_scr, sc_scr, kv_scr, sb_scr,
                rs_ring, smb_ring, a_ring, x_ring, ring_sem):
    b = pl.program_id(0)
    p = pl.program_id(1)
    i = pl.program_id(2)
    nt = pl.num_programs(2)
    t = TILE

    tail_streams = ((rs_hbm, rs_ring), (smb_hbm, smb_ring), (a_hbm, a_ring), (x_hbm, x_ring))

    def tail_copies(tile):
        slot = tile % X_SLOTS
        src_rows = pl.ds(pl.multiple_of(tile * TILE, TILE), TILE)
        return [pltpu.make_async_copy(hbm.at[b, src_rows, :], ring.at[slot],
                                      ring_sem.at[k, slot])
                for k, (hbm, ring) in enumerate(tail_streams)]

    row = lax.broadcasted_iota(jnp.int32, (BLOCK, BLOCK), 0)
    col = lax.broadcasted_iota(jnp.int32, (BLOCK, BLOCK), 1)
    eye = row == col

    def rows(c):
        return slice(c * BLOCK, (c + 1) * BLOCK)

    def hk(h):
        return slice(h * HEAD_K, (h + 1) * HEAD_K)

    def hv(h):
        return slice(h * HEAD_V, (h + 1) * HEAD_V)

    def as_column(r, h):
        d = d_ref[0, 0, r:r + 1, hk(h)]
        return jnp.sum(jnp.where(eye, d, 0.0), axis=1, keepdims=True)

    def scan_tile(q_ref, k_ref, d_row0, mask, order, emit):
        for c in range(BPT):
            for h in range(HEADS):
                q = q_ref[0, rows(c), hk(h)]
                k = k_ref[0, rows(c), hk(h)]
                sc_scr[c, h] = jnp.where(mask, _dot_nt(q, k), 0.0).astype(BF16)
                kv_scr[c, h] = _dot_tn(k, v_ref[0, rows(c), hv(h)])
        for h in range(HEADS):
            state = state_scr[h]
            for c in order:
                s_in = state * as_column(d_row0 + c, h)
                sb_scr[c, h] = s_in.astype(BF16)
                state = (s_in + kv_scr[c, h]) * as_column(d_row0 + BPT + c, h)
            state_scr[h] = state
        for c in range(BPT):
            for h in range(HEADS):
                lhs = jnp.concatenate([q_ref[0, rows(c), hk(h)], sc_scr[c, h]], axis=1)
                rhs = jnp.concatenate([sb_scr[c, h], v_ref[0, rows(c), hv(h)]], axis=0)
                emit(c, h, _dot(lhs, rhs))

    @pl.when(jnp.logical_and(p == 0, i == 0))
    def _():
        state_scr[...] = jnp.zeros_like(state_scr)

    @pl.when(jnp.logical_and(p == 1, i == 0))
    def _():
        state_scr[...] = s0_ref[...]

    @pl.when(p == 0)
    def _():
        base = (nt - 1 - i) * t

        @pl.when(i >= nt - 2)
        def _():
            for cp in tail_copies(i - (nt - 2)):
                cp.start()

        def emit(c, h, o):
            ob_scr[pl.ds(pl.multiple_of(base + c * BLOCK, BLOCK), BLOCK), hv(h)] = o

        scan_tile(qb_ref, kb_ref, 2 * BPT, col > row, list(reversed(range(BPT))), emit)

    @pl.when(p == 1)
    def _():
        base = i * t

        @pl.when(i + 2 < nt)
        def _():
            for cp in tail_copies(i + 2):
                cp.start()

        for cp in tail_copies(i):
            cp.wait()
        slot = i % X_SLOTS

        def emit(c, h, o):
            ob = ob_scr[pl.ds(pl.multiple_of(base + c * BLOCK, BLOCK), BLOCK), hv(h)]
            o_scr[rows(c), hv(h)] = o + ob

        scan_tile(qf_ref, kf_ref, 0, col <= row, list(range(BPT)), emit)
        glag = glag_ref[...]
        ys = []
        for h in range(HEADS):
            oh = _rms(o_scr[:, hv(h)], glag)
            ys.append((oh * rs_ring[slot, :, hv(h)].astype(F32)).astype(BF16))
        y = jnp.concatenate(ys, axis=-1)
        p_gla = _dot(y, wog_ref[...])
        merged = a_ring[slot].astype(F32) + smb_ring[slot].astype(F32) * p_gla
        out = _dot(merged.astype(BF16), wo_ref[...])
        out_ref[0] = x_ring[slot] + _rms(out, gpost_ref[...])


def kernel(x, meta_tokens, norm_pre, w_in, conv_w, w_gate_fwd, b_gate_fwd, w_gate_bwd,
           b_gate_bwd, gla_norm, w_out_conv, w_out_gla, w_merge_out, norm_post):
    bsz, seq, _ = x.shape
    assert seq % TILE == 0 and norm_pre.shape[0] == 1
    nt = seq // TILE
    assert 2 <= nt

    w_main = w_in[0].astype(BF16)
    w_merge = w_main[:, COL_MERGE:]
    w_lr = jnp.pad(w_main[:, COL_LR:COL_MERGE], ((0, 0), (0, LR_PAD - 2 * GATE_RANK)))
    wg = jnp.zeros((LR_PAD, 2 * DK), F32)
    wg = wg.at[:GATE_RANK, :DK].set(w_gate_fwd[0])
    wg = wg.at[GATE_RANK:2 * GATE_RANK, DK:].set(w_gate_bwd[0]).astype(BF16)
    bg = jnp.concatenate([b_gate_fwd[0], b_gate_bwd[0]])[None, :]
    woc = w_out_conv[0].astype(BF16)
    wog = w_out_gla[0].astype(BF16)
    wo = w_merge_out[0].astype(BF16)
    gpre = norm_pre[0][None, :]
    gpost = norm_post[0][None, :]
    glag = gla_norm[0][None, :]
    convw = conv_w[0]

    def col_spec(width, col):
        return pl.BlockSpec((D_MODEL, width), lambda g, c=col // width: (0, c))

    def full1(shape):
        return pl.BlockSpec(shape, lambda g: (0,) * len(shape))

    smeta, s0 = pl.pallas_call(
        _meta_kernel,
        grid=(1,),
        in_specs=[full1((N_META, D_MODEL)), full1((1, D_MODEL)),
                  col_spec(1024, COL_CC), col_spec(1024, COL_CX),
                  col_spec(DK, COL_K), col_spec(DV, COL_V), full1((D_MODEL, LR_PAD)),
                  full1((LR_PAD, 2 * DK)), full1((1, 2 * DK))],
        out_specs=[full1((N_META, D_MODEL)), full1((HEADS, HEAD_K, HEAD_V))],
        out_shape=[jax.ShapeDtypeStruct((N_META, D_MODEL), F32),
                   jax.ShapeDtypeStruct((HEADS, HEAD_K, HEAD_V), F32)],
        compiler_params=pltpu.CompilerParams(vmem_limit_bytes=VMEM_LIMIT),
        name="meta_prologue",
    )(meta_tokens, gpre, w_main, w_main, w_main, w_main, w_lr, wg, bg)

    hb = TILE // HALO
    n_hb = seq // HALO

    def const2(shape):
        return pl.BlockSpec(shape, lambda b, i: (0,) * len(shape),
                            pipeline_mode=pl.Buffered(1))

    def tok_spec(width):
        return pl.BlockSpec((1, TILE, width), lambda b, i: (b, i, 0))

    d_spec = pl.BlockSpec((1, 1, 4 * BPT, DK), lambda b, i: (b, i, 0, 0))
    tok_shape = lambda width: jax.ShapeDtypeStruct((bsz, seq, width), BF16)
    d_shape = jax.ShapeDtypeStruct((bsz, nt, 4 * BPT, DK), F32)

    qf, kf, qb, kb, v, rs, smb, a, d = pl.pallas_call(
        _proj_kernel,
        grid=(bsz, nt),
        in_specs=[
            tok_spec(D_MODEL),
            pl.BlockSpec((1, HALO, D_MODEL),
                         lambda b, i: (b, jnp.maximum(i * hb - 1, 0), 0)),
            pl.BlockSpec((1, HALO, D_MODEL),
                         lambda b, i: (b, jnp.minimum((i + 1) * hb, n_hb - 1), 0)),
            const2((N_META, D_MODEL)), const2((1, D_MODEL)),
            const2((D_MODEL, N_IN)), const2((D_MODEL, 2 * D_MODEL)),
            const2((D_MODEL, LR_PAD)),
            const2((3, D_MODEL)), const2((LR_PAD, 2 * DK)), const2((1, 2 * DK)),
            const2((D_MODEL, D_MODEL)),
        ],
        out_specs=[tok_spec(DK), tok_spec(DK), tok_spec(DK), tok_spec(DK),
                   tok_spec(DV), tok_spec(DV), tok_spec(D_MODEL), tok_spec(D_MODEL),
                   d_spec],
        out_shape=[tok_shape(DK), tok_shape(DK), tok_shape(DK), tok_shape(DK),
                   tok_shape(DV), tok_shape(DV), tok_shape(D_MODEL), tok_shape(D_MODEL),
                   d_shape],
        scratch_shapes=[pltpu.VMEM((TILE + 2 * HALO, D_MODEL), BF16),
                        pltpu.VMEM((TILE + 2 * HALO, CW), F32),
                        pltpu.VMEM((TILE, D_MODEL), BF16),
                        pltpu.VMEM((TILE, 2 * DK), F32),
                        pltpu.VMEM((TILE, DK), F32),
                        pltpu.VMEM((TILE, DK), F32)],
        compiler_params=pltpu.CompilerParams(
            dimension_semantics=("parallel", "arbitrary"),
            vmem_limit_bytes=VMEM_LIMIT),
        name="inproj_conv_gates",
    )(x, x, x, smeta, gpre, w_main, w_merge, w_lr, convw, wg, bg, woc)

    def fwd_map(b, p, i):
        return (b, p * i, 0)

    def bwd_map(b, p, i):
        return (b, (1 - p) * (nt - 1 - i), 0)

    def both_map(b, p, i):
        return (b, p * i + (1 - p) * (nt - 1 - i), 0)

    def tok2(width, imap):
        return pl.BlockSpec((1, TILE, width), imap)

    def const3(shape):
        return pl.BlockSpec(shape, lambda b, p, i: (0,) * len(shape))

    d_spec2 = pl.BlockSpec((1, 1, 4 * BPT, DK),
                           lambda b, p, i: (b, p * i + (1 - p) * (nt - 1 - i), 0, 0))
    hbm_spec = pl.BlockSpec(memory_space=pl.ANY)

    out = pl.pallas_call(
        _gla_kernel,
        grid=(bsz, 2, nt),
        in_specs=[
            tok2(DK, fwd_map), tok2(DK, fwd_map), tok2(DK, bwd_map), tok2(DK, bwd_map),
            tok2(DV, both_map), hbm_spec, hbm_spec, hbm_spec, hbm_spec,
            d_spec2,
            const3((HEADS, HEAD_K, HEAD_V)), const3((1, HEAD_V)),
            const3((DV, D_MODEL)), const3((D_MODEL, D_MODEL)), const3((1, D_MODEL)),
        ],
        out_specs=tok2(D_MODEL, fwd_map),
        out_shape=jax.ShapeDtypeStruct((bsz, seq, D_MODEL), x.dtype),
        scratch_shapes=[pltpu.VMEM((HEADS, HEAD_K, HEAD_V), F32),
                        pltpu.VMEM((seq, DV), F32),
                        pltpu.VMEM((TILE, DV), F32),
                        pltpu.VMEM((BPT, HEADS, BLOCK, BLOCK), BF16),
                        pltpu.VMEM((BPT, HEADS, HEAD_K, HEAD_V), F32),
                        pltpu.VMEM((BPT, HEADS, HEAD_K, HEAD_V), BF16),
                        pltpu.VMEM((X_SLOTS, TILE, DV), BF16),
                        pltpu.VMEM((X_SLOTS, TILE, D_MODEL), BF16),
                        pltpu.VMEM((X_SLOTS, TILE, D_MODEL), BF16),
                        pltpu.VMEM((X_SLOTS, TILE, D_MODEL), F32),
                        pltpu.SemaphoreType.DMA((4, X_SLOTS))],
        compiler_params=pltpu.CompilerParams(
            dimension_semantics=("arbitrary", "arbitrary", "arbitrary"),
            vmem_limit_bytes=VMEM_LIMIT),
        name="gla_merge_out",
    )(qf, kf, qb, kb, v, rs, smb, a, x, d, s0, glag, wog, wo, gpost)
    return out
```

```python
import jax
import jax.numpy as jnp
from jax import lax
from jax.experimental import pallas as pl
from jax.experimental.pallas import tpu as pltpu

D_MODEL = 1024
N_META = 16
HEADS = 4
DK = 512
DV = 1024
HEAD_K = DK // HEADS
HEAD_V = DV // HEADS
GATE_RANK = 16
GATE_NORMALIZER = 16.0
EPS = 1e-6

COL_CB, COL_CC, COL_CX, COL_CZ = 0, 1024, 2048, 3072
COL_Q, COL_K, COL_V, COL_R = 4096, 4608, 5120, 6144
COL_LR = 7168
COL_MERGE = COL_LR + 2 * GATE_RANK
N_IN = COL_MERGE + 2 * D_MODEL
LR_PAD = 128

HALO = 16
TILE = 512
BLOCK = 128
BPT = TILE // BLOCK
CW = 256
VMEM_LIMIT = 56 * 1024 * 1024

F32 = jnp.float32
BF16 = jnp.bfloat16


def _dot(a, b):
    return jnp.dot(a, b, preferred_element_type=F32)


def _dot_nt(a, b):
    return lax.dot_general(a, b, (((1,), (1,)), ((), ())), preferred_element_type=F32)


def _dot_tn(a, b):
    return lax.dot_general(a, b, (((0,), (0,)), ((), ())), preferred_element_type=F32)


def _rms(x, g):
    ms = jnp.mean(x * x, axis=-1, keepdims=True)
    return x * lax.rsqrt(ms + EPS) * g


def _sigmoid(x):
    return 1.0 / (1.0 + jnp.exp(-x))


def _log_sigmoid(x):
    return jnp.minimum(x, 0.0) - jnp.log1p(jnp.exp(-jnp.abs(x)))


def _sum_matrix(n, ref, reverse):
    r = lax.broadcasted_iota(jnp.int32, (n, 2 * n), 0)
    c = lax.broadcasted_iota(jnp.int32, (n, 2 * n), 1)
    c = jnp.where(c >= n, c - n, c)
    if reverse:
        m = jnp.where(c >= r, 1.0, 0.0) - jnp.where(c >= ref, 1.0, 0.0)
    else:
        m = jnp.where(c <= r, 1.0, 0.0) - jnp.where(c < ref, 1.0, 0.0)
    return m.astype(BF16)


def _row_sums(mat2, g):
    hi = g.astype(BF16)
    lo = (g - hi.astype(F32)).astype(BF16)
    return _dot(mat2, jnp.concatenate([hi, lo], axis=0))


def _block(c):
    return slice(c * BLOCK, (c + 1) * BLOCK)


def _hk(h):
    return slice(h * HEAD_K, (h + 1) * HEAD_K)


def _hv(h):
    return slice(h * HEAD_V, (h + 1) * HEAD_V)


def _gla_tile(q_at, k_at, v_at, d_at, backward, state_scr, sc_scr, kv_scr, sb_scr, emit):
    row = lax.broadcasted_iota(jnp.int32, (BLOCK, BLOCK), 0)
    col = lax.broadcasted_iota(jnp.int32, (BLOCK, BLOCK), 1)
    eye = row == col
    mask = (col > row) if backward else (col <= row)
    order = list(reversed(range(BPT))) if backward else list(range(BPT))

    def as_column(d):
        return jnp.sum(jnp.where(eye, d, 0.0), axis=1, keepdims=True)

    for c in range(BPT):
        for h in range(HEADS):
            k = k_at(c, h)
            sc_scr[c, h] = jnp.where(mask, _dot_nt(q_at(c, h), k), 0.0).astype(BF16)
            kv_scr[c, h] = _dot_tn(k, v_at(c, h))
    for h in range(HEADS):
        state = state_scr[h]
        for c in order:
            s_in = state * as_column(d_at(c, h))
            sb_scr[c, h] = s_in.astype(BF16)
            state = (s_in + kv_scr[c, h]) * as_column(d_at(BPT + c, h))
        state_scr[h] = state
    for c in range(BPT):
        for h in range(HEADS):
            lhs = jnp.concatenate([q_at(c, h), sc_scr[c, h]], axis=1)
            rhs = jnp.concatenate([sb_scr[c, h], v_at(c, h)], axis=0)
            emit(c, h, _dot(lhs, rhs))


def _meta_kernel(meta_ref, gpre_ref, wcc_ref, wcx_ref, wk_ref, wv_ref, wlr_ref,
                 wg_ref, bg_ref, smeta_ref, s0_ref):
    u = _rms(meta_ref[...], gpre_ref[...]).astype(BF16)
    smeta_ref[...] = _dot(u, wcc_ref[...]) * _dot(u, wcx_ref[...])
    k = _dot(u, wk_ref[...])
    v = _dot(u, wv_ref[...]).astype(BF16)
    lr = _dot(u, wlr_ref[...]).astype(BF16)
    z = _dot(lr, wg_ref[:, :DK]) + bg_ref[:, :DK]
    g = _log_sigmoid(z) * (1.0 / GATE_NORMALIZER)
    b = _row_sums(_sum_matrix(N_META, 0, reverse=False), g)
    kdec = (k * jnp.exp(b[N_META - 1:N_META, :] - b)).astype(BF16)
    for h in range(HEADS):
        s0_ref[h] = _dot_tn(kdec[:, h * HEAD_K:(h + 1) * HEAD_K],
                            v[:, h * HEAD_V:(h + 1) * HEAD_V])


def _proj_kernel(x_ref, xp_ref, xn_ref, smeta_ref, gpre_ref, w_ref, wm_ref, wlr_ref,
                 convw_ref, wg_ref, bg_ref, woc_ref,
                 qf_ref, kf_ref, v_ref, rs_ref, smb_ref, a_ref, d_ref, ob_ref,
                 u_scr, s_scr, y_scr, g_scr, q_scr, k_scr, qb_scr, kb_scr,
                 state_scr, sc_scr, kv_scr, sb_scr):
    i = pl.program_id(1)
    last = pl.num_programs(1) - 1
    t = TILE
    gpre = gpre_ref[...]

    @pl.when(i == 0)
    def _():
        state_scr[...] = jnp.zeros_like(state_scr)

    u_scr[0:HALO, :] = _rms(xp_ref[0], gpre).astype(BF16)
    u_scr[HALO:HALO + t, :] = _rms(x_ref[0], gpre).astype(BF16)
    u_scr[HALO + t:, :] = _rms(xn_ref[0], gpre).astype(BF16)

    def w(col, width=CW):
        return w_ref[:, col:col + width]

    u_main = u_scr[HALO:HALO + t, :]
    lr = _dot(u_main, wlr_ref[...]).astype(BF16)
    g_scr[...] = _log_sigmoid(_dot(lr, wg_ref[...]) + bg_ref[...]) * (1.0 / GATE_NORMALIZER)
    q_scr[...] = _dot(u_main, w(COL_Q, DK)) * (HEAD_K ** -0.5)
    k_scr[...] = _dot(u_main, w(COL_K, DK))

    half = BLOCK // 2
    mat_f = _sum_matrix(BLOCK, half, reverse=False)
    mat_b = _sum_matrix(BLOCK, half, reverse=True)

    def decay_block(c):
        rs = slice(c * BLOCK, (c + 1) * BLOCK)
        qc = q_scr[rs, :]
        kc = k_scr[rs, :]
        gf = g_scr[rs, :DK]
        gb = g_scr[rs, DK:]
        bf = _row_sums(mat_f, gf)
        bb = _row_sums(mat_b, gb)
        qf_ref[0, rs, :] = (qc * jnp.exp(bf)).astype(BF16)
        kf_ref[0, rs, :] = (kc * jnp.exp(-bf)).astype(BF16)
        qb_scr[rs, :] = (qc * jnp.exp(bb)).astype(BF16)
        kb_scr[rs, :] = (kc * jnp.exp(-bb)).astype(BF16)
        d_ref[0, 0, c:c + 1, :] = jnp.exp(gf[0:1, :] - bf[0:1, :])
        d_ref[0, 0, BPT + c:BPT + c + 1, :] = jnp.exp(bf[BLOCK - 1:BLOCK, :])
        d_ref[0, 0, 2 * BPT + c:2 * BPT + c + 1, :] = jnp.exp(
            gb[BLOCK - 1:BLOCK, :] - bb[BLOCK - 1:BLOCK, :])
        d_ref[0, 0, 3 * BPT + c:3 * BPT + c + 1, :] = jnp.exp(bb[0:1, :])

    assert D_MODEL // CW == BPT
    for j in range(D_MODEL // CW):
        c0 = j * CW
        u_ext = u_scr[...]
        s_scr[...] = _dot(u_ext, w(COL_CC + c0)) * _dot(u_ext, w(COL_CX + c0))
        s_scr[HALO - 1:HALO, :] = jnp.where(
            i == last, smeta_ref[N_META - 1:N_META, c0:c0 + CW], s_scr[HALO - 1:HALO, :])
        s_scr[HALO + t:HALO + t + 1, :] = jnp.where(
            i == 0, 0.0, s_scr[HALO + t:HALO + t + 1, :])
        conv = (s_scr[HALO - 1:HALO - 1 + t, :] * convw_ref[0:1, c0:c0 + CW]
                + s_scr[HALO:HALO + t, :] * convw_ref[1:2, c0:c0 + CW]
                + s_scr[HALO + 1:HALO + 1 + t, :] * convw_ref[2:3, c0:c0 + CW])
        cb = _dot(u_main, w(COL_CB + c0))
        cz = _dot(u_main, w(COL_CZ + c0))
        y_scr[:, c0:c0 + CW] = (cb * conv * (cz * _sigmoid(cz))).astype(BF16)
        decay_block(j)

    for j in range(D_MODEL // CW):
        c0 = j * CW
        pc = _dot(y_scr[...], woc_ref[:, c0:c0 + CW])
        ma = _dot(u_main, wm_ref[:, c0:c0 + CW])
        a_ref[0, :, c0:c0 + CW] = (_sigmoid(ma) * pc).astype(BF16)
        mb = _dot(u_main, wm_ref[:, D_MODEL + c0:D_MODEL + c0 + CW])
        smb_ref[0, :, c0:c0 + CW] = _sigmoid(mb).astype(BF16)
        r = _dot(u_main, w(COL_R + c0))
        rs_ref[0, :, c0:c0 + CW] = (r * _sigmoid(r)).astype(BF16)
        v_ref[0, :, c0:c0 + CW] = _dot(u_main, w(COL_V + c0)).astype(BF16)

    def emit(c, h, o):
        ob_ref[0, _block(c), _hv(h)] = o.astype(BF16)

    _gla_tile(lambda c, h: qb_scr[_block(c), _hk(h)],
              lambda c, h: kb_scr[_block(c), _hk(h)],
              lambda c, h: v_ref[0, _block(c), _hv(h)],
              lambda r, h: d_ref[0, 0, 2 * BPT + r:2 * BPT + r + 1, _hk(h)],
              True, state_scr, sc_scr, kv_scr, sb_scr, emit)


def _gla_kernel(qf_ref, kf_ref, v_ref, ob_ref, rs_ref, smb_ref, a_ref, x_ref,
                d_ref, s0_ref, glag_ref, wog_ref, wo_ref, gpost_ref,
                out_ref, state_scr, o_scr, sc_scr, kv_scr, sb_scr):
    i = pl.program_id(1)

    @pl.when(i == 0)
    def _():
        state_scr[...] = s0_ref[...]

    def emit(c, h, o):
        o_scr[_block(c), _hv(h)] = o + ob_ref[0, _block(c), _hv(h)].astype(F32)

    _gla_tile(lambda c, h: qf_ref[0, _block(c), _hk(h)],
              lambda c, h: kf_ref[0, _block(c), _hk(h)],
              lambda c, h: v_ref[0, _block(c), _hv(h)],
              lambda r, h: d_ref[0, 0, r:r + 1, _hk(h)],
              False, state_scr, sc_scr, kv_scr, sb_scr, emit)
    glag = glag_ref[...]
    ys = []
    for h in range(HEADS):
        oh = _rms(o_scr[:, _hv(h)], glag)
        ys.append((oh * rs_ref[0, :, _hv(h)].astype(F32)).astype(BF16))
    y = jnp.concatenate(ys, axis=-1)
    p_gla = _dot(y, wog_ref[...])
    merged = a_ref[0].astype(F32) + smb_ref[0].astype(F32) * p_gla
    out = _dot(merged.astype(BF16), wo_ref[...])
    out_ref[0] = x_ref[0] + _rms(out, gpost_ref[...])


def kernel(x, meta_tokens, norm_pre, w_in, conv_w, w_gate_fwd, b_gate_fwd, w_gate_bwd,
           b_gate_bwd, gla_norm, w_out_conv, w_out_gla, w_merge_out, norm_post):
    bsz, seq, _ = x.shape
    assert seq % TILE == 0 and norm_pre.shape[0] == 1
    nt = seq // TILE

    w_main = w_in[0].astype(BF16)
    w_merge = w_main[:, COL_MERGE:]
    w_lr = jnp.pad(w_main[:, COL_LR:COL_MERGE], ((0, 0), (0, LR_PAD - 2 * GATE_RANK)))
    wg = jnp.zeros((LR_PAD, 2 * DK), F32)
    wg = wg.at[:GATE_RANK, :DK].set(w_gate_fwd[0])
    wg = wg.at[GATE_RANK:2 * GATE_RANK, DK:].set(w_gate_bwd[0]).astype(BF16)
    bg = jnp.concatenate([b_gate_fwd[0], b_gate_bwd[0]])[None, :]
    woc = w_out_conv[0].astype(BF16)
    wog = w_out_gla[0].astype(BF16)
    wo = w_merge_out[0].astype(BF16)
    gpre = norm_pre[0][None, :]
    gpost = norm_post[0][None, :]
    glag = gla_norm[0][None, :]
    convw = conv_w[0]

    def col_spec(width, col):
        return pl.BlockSpec((D_MODEL, width), lambda g, c=col // width: (0, c))

    def full1(shape):
        return pl.BlockSpec(shape, lambda g: (0,) * len(shape))

    smeta, s0 = pl.pallas_call(
        _meta_kernel,
        grid=(1,),
        in_specs=[full1((N_META, D_MODEL)), full1((1, D_MODEL)),
                  col_spec(1024, COL_CC), col_spec(1024, COL_CX),
                  col_spec(DK, COL_K), col_spec(DV, COL_V), full1((D_MODEL, LR_PAD)),
                  full1((LR_PAD, 2 * DK)), full1((1, 2 * DK))],
        out_specs=[full1((N_META, D_MODEL)), full1((HEADS, HEAD_K, HEAD_V))],
        out_shape=[jax.ShapeDtypeStruct((N_META, D_MODEL), F32),
                   jax.ShapeDtypeStruct((HEADS, HEAD_K, HEAD_V), F32)],
        compiler_params=pltpu.CompilerParams(vmem_limit_bytes=VMEM_LIMIT),
        name="meta_prologue",
    )(meta_tokens, gpre, w_main, w_main, w_main, w_main, w_lr, wg, bg)

    hb = TILE // HALO
    n_hb = seq // HALO

    def const2(shape):
        return pl.BlockSpec(shape, lambda b, i: (0,) * len(shape),
                            pipeline_mode=pl.Buffered(1))

    def tok_spec(width):
        return pl.BlockSpec((1, TILE, width), lambda b, i: (b, nt - 1 - i, 0))

    d_spec = pl.BlockSpec((1, 1, 4 * BPT, DK), lambda b, i: (b, nt - 1 - i, 0, 0))
    tok_shape = lambda width: jax.ShapeDtypeStruct((bsz, seq, width), BF16)
    d_shape = jax.ShapeDtypeStruct((bsz, nt, 4 * BPT, DK), F32)

    qf, kf, v, rs, smb, a, d, ob = pl.pallas_call(
        _proj_kernel,
        grid=(bsz, nt),
        in_specs=[
            tok_spec(D_MODEL),
            pl.BlockSpec((1, HALO, D_MODEL),
                         lambda b, i: (b, jnp.maximum((nt - 1 - i) * hb - 1, 0), 0)),
            pl.BlockSpec((1, HALO, D_MODEL),
                         lambda b, i: (b, jnp.minimum((nt - i) * hb, n_hb - 1), 0)),
            const2((N_META, D_MODEL)), const2((1, D_MODEL)),
            const2((D_MODEL, N_IN)), const2((D_MODEL, 2 * D_MODEL)),
            const2((D_MODEL, LR_PAD)),
            const2((3, D_MODEL)), const2((LR_PAD, 2 * DK)), const2((1, 2 * DK)),
            const2((D_MODEL, D_MODEL)),
        ],
        out_specs=[tok_spec(DK), tok_spec(DK),
                   tok_spec(DV), tok_spec(DV), tok_spec(D_MODEL), tok_spec(D_MODEL),
                   d_spec, tok_spec(DV)],
        out_shape=[tok_shape(DK), tok_shape(DK),
                   tok_shape(DV), tok_shape(DV), tok_shape(D_MODEL), tok_shape(D_MODEL),
                   d_shape, tok_shape(DV)],
        scratch_shapes=[pltpu.VMEM((TILE + 2 * HALO, D_MODEL), BF16),
                        pltpu.VMEM((TILE + 2 * HALO, CW), F32),
                        pltpu.VMEM((TILE, D_MODEL), BF16),
                        pltpu.VMEM((TILE, 2 * DK), F32),
                        pltpu.VMEM((TILE, DK), F32),
                        pltpu.VMEM((TILE, DK), F32),
                        pltpu.VMEM((TILE, DK), BF16),
                        pltpu.VMEM((TILE, DK), BF16),
                        pltpu.VMEM((HEADS, HEAD_K, HEAD_V), F32),
                        pltpu.VMEM((BPT, HEADS, BLOCK, BLOCK), BF16),
                        pltpu.VMEM((BPT, HEADS, HEAD_K, HEAD_V), F32),
                        pltpu.VMEM((BPT, HEADS, HEAD_K, HEAD_V), BF16)],
        compiler_params=pltpu.CompilerParams(
            dimension_semantics=("parallel", "arbitrary"),
            vmem_limit_bytes=VMEM_LIMIT),
        name="inproj_conv_gates",
    )(x, x, x, smeta, gpre, w_main, w_merge, w_lr, convw, wg, bg, woc)

    def tok2(width):
        return pl.BlockSpec((1, TILE, width), lambda b, i: (b, i, 0))

    def const2b(shape):
        return pl.BlockSpec(shape, lambda b, i: (0,) * len(shape))

    out = pl.pallas_call(
        _gla_kernel,
        grid=(bsz, nt),
        in_specs=[
            tok2(DK), tok2(DK), tok2(DV), tok2(DV), tok2(DV), tok2(D_MODEL),
            tok2(D_MODEL), tok2(D_MODEL),
            pl.BlockSpec((1, 1, 4 * BPT, DK), lambda b, i: (b, i, 0, 0)),
            const2b((HEADS, HEAD_K, HEAD_V)), const2b((1, HEAD_V)),
            const2b((DV, D_MODEL)), const2b((D_MODEL, D_MODEL)), const2b((1, D_MODEL)),
        ],
        out_specs=tok2(D_MODEL),
        out_shape=jax.ShapeDtypeStruct((bsz, seq, D_MODEL), x.dtype),
        scratch_shapes=[pltpu.VMEM((HEADS, HEAD_K, HEAD_V), F32),
                        pltpu.VMEM((TILE, DV), F32),
                        pltpu.VMEM((BPT, HEADS, BLOCK, BLOCK), BF16),
                        pltpu.VMEM((BPT, HEADS, HEAD_K, HEAD_V), F32),
                        pltpu.VMEM((BPT, HEADS, HEAD_K, HEAD_V), BF16)],
        compiler_params=pltpu.CompilerParams(
            dimension_semantics=("parallel", "arbitrary"),
            vmem_limit_bytes=VMEM_LIMIT),
        name="gla_merge_out",
    )(qf, kf, v, ob, rs, smb, a, x, d, s0, glag, wog, wo, gpost)
    return out
```

```python
import jax
import jax.numpy as jnp
from jax import lax
from jax.experimental import pallas as pl
from jax.experimental.pallas import tpu as pltpu

D_MODEL = 1024
N_META = 16
HEADS = 4
DK = 512
DV = 1024
HEAD_K = DK // HEADS
HEAD_V = DV // HEADS
GATE_RANK = 16
GATE_NORMALIZER = 16.0
EPS = 1e-6

COL_CB, COL_CC, COL_CX, COL_CZ = 0, 1024, 2048, 3072
COL_Q, COL_K, COL_V, COL_R = 4096, 4608, 5120, 6144
COL_LR = 7168
COL_MERGE = COL_LR + 2 * GATE_RANK
N_IN = COL_MERGE + 2 * D_MODEL
LR_PAD = 128

HALO = 16
TILE = 512
BLOCK = 128
BPT = TILE // BLOCK
SUBS = 2
CW = 256
VMEM_LIMIT = 56 * 1024 * 1024

F32 = jnp.float32
BF16 = jnp.bfloat16


def _dot(a, b):
    return jnp.dot(a, b, preferred_element_type=F32)


def _dot_nt(a, b):
    return lax.dot_general(a, b, (((1,), (1,)), ((), ())), preferred_element_type=F32)


def _dot_tn(a, b):
    return lax.dot_general(a, b, (((0,), (0,)), ((), ())), preferred_element_type=F32)


def _rms(x, g):
    ms = jnp.mean(x * x, axis=-1, keepdims=True)
    return x * lax.rsqrt(ms + EPS) * g


def _sigmoid(x):
    return 1.0 / (1.0 + jnp.exp(-x))


def _log_sigmoid(x):
    return jnp.minimum(x, 0.0) - jnp.log1p(jnp.exp(-jnp.abs(x)))


def _sum_matrix(n, ref, reverse):
    r = lax.broadcasted_iota(jnp.int32, (n, 2 * n), 0)
    c = lax.broadcasted_iota(jnp.int32, (n, 2 * n), 1)
    c = jnp.where(c >= n, c - n, c)
    if reverse:
        m = jnp.where(c >= r, 1.0, 0.0) - jnp.where(c >= ref, 1.0, 0.0)
    else:
        m = jnp.where(c <= r, 1.0, 0.0) - jnp.where(c < ref, 1.0, 0.0)
    return m.astype(BF16)


def _row_sums(mat2, g):
    hi = g.astype(BF16)
    lo = (g - hi.astype(F32)).astype(BF16)
    return _dot(mat2, jnp.concatenate([hi, lo], axis=0))


def _block(c):
    return slice(c * BLOCK, (c + 1) * BLOCK)


def _hk(h):
    return slice(h * HEAD_K, (h + 1) * HEAD_K)


def _hv(h):
    return slice(h * HEAD_V, (h + 1) * HEAD_V)


def _gla_tile(q_at, k_at, v_at, d_at, backward, state_scr, sc_scr, kv_scr, sb_scr, emit):
    row = lax.broadcasted_iota(jnp.int32, (BLOCK, BLOCK), 0)
    col = lax.broadcasted_iota(jnp.int32, (BLOCK, BLOCK), 1)
    eye = row == col
    mask = (col > row) if backward else (col <= row)
    order = list(reversed(range(BPT))) if backward else list(range(BPT))

    def as_column(d):
        return jnp.sum(jnp.where(eye, d, 0.0), axis=1, keepdims=True)

    for c in range(BPT):
        for h in range(HEADS):
            k = k_at(c, h)
            sc_scr[c, h] = jnp.where(mask, _dot_nt(q_at(c, h), k), 0.0).astype(BF16)
            kv_scr[c, h] = _dot_tn(k, v_at(c, h))
    for h in range(HEADS):
        state = state_scr[h]
        for c in order:
            s_in = state * as_column(d_at(c, h))
            sb_scr[c, h] = s_in.astype(BF16)
            state = (s_in + kv_scr[c, h]) * as_column(d_at(BPT + c, h))
        state_scr[h] = state
    for c in range(BPT):
        for h in range(HEADS):
            lhs = jnp.concatenate([q_at(c, h), sc_scr[c, h]], axis=1)
            rhs = jnp.concatenate([sb_scr[c, h], v_at(c, h)], axis=0)
            emit(c, h, _dot(lhs, rhs))


def _meta_kernel(meta_ref, gpre_ref, wcc_ref, wcx_ref, wk_ref, wv_ref, wlr_ref,
                 wg_ref, bg_ref, smeta_ref, s0_ref):
    u = _rms(meta_ref[...], gpre_ref[...]).astype(BF16)
    smeta_ref[...] = _dot(u, wcc_ref[...]) * _dot(u, wcx_ref[...])
    k = _dot(u, wk_ref[...])
    v = _dot(u, wv_ref[...]).astype(BF16)
    lr = _dot(u, wlr_ref[...]).astype(BF16)
    z = _dot(lr, wg_ref[:, :DK]) + bg_ref[:, :DK]
    g = _log_sigmoid(z) * (1.0 / GATE_NORMALIZER)
    b = _row_sums(_sum_matrix(N_META, 0, reverse=False), g)
    kdec = (k * jnp.exp(b[N_META - 1:N_META, :] - b)).astype(BF16)
    for h in range(HEADS):
        s0_ref[h] = _dot_tn(kdec[:, h * HEAD_K:(h + 1) * HEAD_K],
                            v[:, h * HEAD_V:(h + 1) * HEAD_V])


def _proj_kernel(x_ref, xp_ref, xn_ref, smeta_ref, gpre_ref, w_ref, wm_ref, wlr_ref,
                 convw_ref, wg_ref, bg_ref, woc_ref,
                 qf_ref, kf_ref, v_ref, rs_ref, smb_ref, a_ref, d_ref, ob_ref,
                 u_scr, s_scr, y_scr, g_scr, q_scr, k_scr, qb_scr, kb_scr,
                 state_scr, sc_scr, kv_scr, sb_scr):
    i = pl.program_id(1)
    last = pl.num_programs(1) - 1
    t = TILE
    gpre = gpre_ref[...]

    @pl.when(i == 0)
    def _():
        state_scr[...] = jnp.zeros_like(state_scr)

    u_scr[0:HALO, :] = _rms(xp_ref[0], gpre).astype(BF16)
    u_scr[HALO:HALO + t, :] = _rms(x_ref[0], gpre).astype(BF16)
    u_scr[HALO + t:, :] = _rms(xn_ref[0], gpre).astype(BF16)

    def w(col, width=CW):
        return w_ref[:, col:col + width]

    u_main = u_scr[HALO:HALO + t, :]
    lr = _dot(u_main, wlr_ref[...]).astype(BF16)
    g_scr[...] = _log_sigmoid(_dot(lr, wg_ref[...]) + bg_ref[...]) * (1.0 / GATE_NORMALIZER)
    q_scr[...] = _dot(u_main, w(COL_Q, DK)) * (HEAD_K ** -0.5)
    k_scr[...] = _dot(u_main, w(COL_K, DK))

    half = BLOCK // 2
    mat_f = _sum_matrix(BLOCK, half, reverse=False)
    mat_b = _sum_matrix(BLOCK, half, reverse=True)

    def decay_block(c):
        rs = slice(c * BLOCK, (c + 1) * BLOCK)
        qc = q_scr[rs, :]
        kc = k_scr[rs, :]
        gf = g_scr[rs, :DK]
        gb = g_scr[rs, DK:]
        bf = _row_sums(mat_f, gf)
        bb = _row_sums(mat_b, gb)
        qf_ref[0, rs, :] = (qc * jnp.exp(bf)).astype(BF16)
        kf_ref[0, rs, :] = (kc * jnp.exp(-bf)).astype(BF16)
        qb_scr[rs, :] = (qc * jnp.exp(bb)).astype(BF16)
        kb_scr[rs, :] = (kc * jnp.exp(-bb)).astype(BF16)
        d_ref[0, 0, c:c + 1, :] = jnp.exp(gf[0:1, :] - bf[0:1, :])
        d_ref[0, 0, BPT + c:BPT + c + 1, :] = jnp.exp(bf[BLOCK - 1:BLOCK, :])
        d_ref[0, 0, 2 * BPT + c:2 * BPT + c + 1, :] = jnp.exp(
            gb[BLOCK - 1:BLOCK, :] - bb[BLOCK - 1:BLOCK, :])
        d_ref[0, 0, 3 * BPT + c:3 * BPT + c + 1, :] = jnp.exp(bb[0:1, :])

    assert D_MODEL // CW == BPT
    for j in range(D_MODEL // CW):
        c0 = j * CW
        u_ext = u_scr[...]
        s_scr[...] = _dot(u_ext, w(COL_CC + c0)) * _dot(u_ext, w(COL_CX + c0))
        s_scr[HALO - 1:HALO, :] = jnp.where(
            i == last, smeta_ref[N_META - 1:N_META, c0:c0 + CW], s_scr[HALO - 1:HALO, :])
        s_scr[HALO + t:HALO + t + 1, :] = jnp.where(
            i == 0, 0.0, s_scr[HALO + t:HALO + t + 1, :])
        conv = (s_scr[HALO - 1:HALO - 1 + t, :] * convw_ref[0:1, c0:c0 + CW]
                + s_scr[HALO:HALO + t, :] * convw_ref[1:2, c0:c0 + CW]
                + s_scr[HALO + 1:HALO + 1 + t, :] * convw_ref[2:3, c0:c0 + CW])
        cb = _dot(u_main, w(COL_CB + c0))
        cz = _dot(u_main, w(COL_CZ + c0))
        y_scr[:, c0:c0 + CW] = (cb * conv * (cz * _sigmoid(cz))).astype(BF16)
        decay_block(j)

    for j in range(D_MODEL // CW):
        c0 = j * CW
        pc = _dot(y_scr[...], woc_ref[:, c0:c0 + CW])
        ma = _dot(u_main, wm_ref[:, c0:c0 + CW])
        a_ref[0, :, c0:c0 + CW] = (_sigmoid(ma) * pc).astype(BF16)
        mb = _dot(u_main, wm_ref[:, D_MODEL + c0:D_MODEL + c0 + CW])
        smb_ref[0, :, c0:c0 + CW] = _sigmoid(mb).astype(BF16)
        r = _dot(u_main, w(COL_R + c0))
        rs_ref[0, :, c0:c0 + CW] = (r * _sigmoid(r)).astype(BF16)
        v_ref[0, :, c0:c0 + CW] = _dot(u_main, w(COL_V + c0)).astype(BF16)

    def emit(c, h, o):
        ob_ref[0, _block(c), _hv(h)] = o.astype(BF16)

    _gla_tile(lambda c, h: qb_scr[_block(c), _hk(h)],
              lambda c, h: kb_scr[_block(c), _hk(h)],
              lambda c, h: v_ref[0, _block(c), _hv(h)],
              lambda r, h: d_ref[0, 0, 2 * BPT + r:2 * BPT + r + 1, _hk(h)],
              True, state_scr, sc_scr, kv_scr, sb_scr, emit)


def _gla_kernel(qf_ref, kf_ref, v_ref, ob_ref, rs_ref, smb_ref, a_ref, x_ref,
                d_ref, s0_ref, glag_ref, wog_ref, wo_ref, gpost_ref,
                out_ref, state_scr, o_scr, sc_scr, kv_scr, sb_scr):
    i = pl.program_id(1)

    @pl.when(i == 0)
    def _():
        state_scr[...] = s0_ref[...]

    for sub in range(SUBS):
        def rows(c, r0=sub * TILE):
            return slice(r0 + c * BLOCK, r0 + (c + 1) * BLOCK)

        def emit(c, h, o, rows=rows):
            o_scr[_block(c), _hv(h)] = o + ob_ref[0, rows(c), _hv(h)].astype(F32)

        _gla_tile(lambda c, h, rows=rows: qf_ref[0, rows(c), _hk(h)],
                  lambda c, h, rows=rows: kf_ref[0, rows(c), _hk(h)],
                  lambda c, h, rows=rows: v_ref[0, rows(c), _hv(h)],
                  lambda r, h, sub=sub: d_ref[0, sub, r:r + 1, _hk(h)],
                  False, state_scr, sc_scr, kv_scr, sb_scr, emit)
        tsl = slice(sub * TILE, (sub + 1) * TILE)
        glag = glag_ref[...]
        ys = []
        for h in range(HEADS):
            oh = _rms(o_scr[:, _hv(h)], glag)
            ys.append((oh * rs_ref[0, tsl, _hv(h)].astype(F32)).astype(BF16))
        y = jnp.concatenate(ys, axis=-1)
        p_gla = _dot(y, wog_ref[...])
        merged = a_ref[0, tsl, :].astype(F32) + smb_ref[0, tsl, :].astype(F32) * p_gla
        out = _dot(merged.astype(BF16), wo_ref[...])
        out_ref[0, tsl, :] = x_ref[0, tsl, :] + _rms(out, gpost_ref[...])


def kernel(x, meta_tokens, norm_pre, w_in, conv_w, w_gate_fwd, b_gate_fwd, w_gate_bwd,
           b_gate_bwd, gla_norm, w_out_conv, w_out_gla, w_merge_out, norm_post):
    bsz, seq, _ = x.shape
    assert seq % TILE == 0 and norm_pre.shape[0] == 1
    nt = seq // TILE

    w_main = w_in[0].astype(BF16)
    w_merge = w_main[:, COL_MERGE:]
    w_lr = jnp.pad(w_main[:, COL_LR:COL_MERGE], ((0, 0), (0, LR_PAD - 2 * GATE_RANK)))
    wg = jnp.zeros((LR_PAD, 2 * DK), F32)
    wg = wg.at[:GATE_RANK, :DK].set(w_gate_fwd[0])
    wg = wg.at[GATE_RANK:2 * GATE_RANK, DK:].set(w_gate_bwd[0]).astype(BF16)
    bg = jnp.concatenate([b_gate_fwd[0], b_gate_bwd[0]])[None, :]
    woc = w_out_conv[0].astype(BF16)
    wog = w_out_gla[0].astype(BF16)
    wo = w_merge_out[0].astype(BF16)
    gpre = norm_pre[0][None, :]
    gpost = norm_post[0][None, :]
    glag = gla_norm[0][None, :]
    convw = conv_w[0]

    def col_spec(width, col):
        return pl.BlockSpec((D_MODEL, width), lambda g, c=col // width: (0, c))

    def full1(shape):
        return pl.BlockSpec(shape, lambda g: (0,) * len(shape))

    smeta, s0 = pl.pallas_call(
        _meta_kernel,
        grid=(1,),
        in_specs=[full1((N_META, D_MODEL)), full1((1, D_MODEL)),
                  col_spec(1024, COL_CC), col_spec(1024, COL_CX),
                  col_spec(DK, COL_K), col_spec(DV, COL_V), full1((D_MODEL, LR_PAD)),
                  full1((LR_PAD, 2 * DK)), full1((1, 2 * DK))],
        out_specs=[full1((N_META, D_MODEL)), full1((HEADS, HEAD_K, HEAD_V))],
        out_shape=[jax.ShapeDtypeStruct((N_META, D_MODEL), F32),
                   jax.ShapeDtypeStruct((HEADS, HEAD_K, HEAD_V), F32)],
        compiler_params=pltpu.CompilerParams(vmem_limit_bytes=VMEM_LIMIT),
        name="meta_prologue",
    )(meta_tokens, gpre, w_main, w_main, w_main, w_main, w_lr, wg, bg)

    hb = TILE // HALO
    n_hb = seq // HALO

    def const2(shape):
        return pl.BlockSpec(shape, lambda b, i: (0,) * len(shape),
                            pipeline_mode=pl.Buffered(1))

    def tok_spec(width):
        return pl.BlockSpec((1, TILE, width), lambda b, i: (b, nt - 1 - i, 0))

    d_spec = pl.BlockSpec((1, 1, 4 * BPT, DK), lambda b, i: (b, nt - 1 - i, 0, 0))
    tok_shape = lambda width: jax.ShapeDtypeStruct((bsz, seq, width), BF16)
    d_shape = jax.ShapeDtypeStruct((bsz, nt, 4 * BPT, DK), F32)

    qf, kf, v, rs, smb, a, d, ob = pl.pallas_call(
        _proj_kernel,
        grid=(bsz, nt),
        in_specs=[
            tok_spec(D_MODEL),
            pl.BlockSpec((1, HALO, D_MODEL),
                         lambda b, i: (b, jnp.maximum((nt - 1 - i) * hb - 1, 0), 0)),
            pl.BlockSpec((1, HALO, D_MODEL),
                         lambda b, i: (b, jnp.minimum((nt - i) * hb, n_hb - 1), 0)),
            const2((N_META, D_MODEL)), const2((1, D_MODEL)),
            const2((D_MODEL, N_IN)), const2((D_MODEL, 2 * D_MODEL)),
            const2((D_MODEL, LR_PAD)),
            const2((3, D_MODEL)), const2((LR_PAD, 2 * DK)), const2((1, 2 * DK)),
            const2((D_MODEL, D_MODEL)),
        ],
        out_specs=[tok_spec(DK), tok_spec(DK),
                   tok_spec(DV), tok_spec(DV), tok_spec(D_MODEL), tok_spec(D_MODEL),
                   d_spec, tok_spec(DV)],
        out_shape=[tok_shape(DK), tok_shape(DK),
                   tok_shape(DV), tok_shape(DV), tok_shape(D_MODEL), tok_shape(D_MODEL),
                   d_shape, tok_shape(DV)],
        scratch_shapes=[pltpu.VMEM((TILE + 2 * HALO, D_MODEL), BF16),
                        pltpu.VMEM((TILE + 2 * HALO, CW), F32),
                        pltpu.VMEM((TILE, D_MODEL), BF16),
                        pltpu.VMEM((TILE, 2 * DK), F32),
                        pltpu.VMEM((TILE, DK), F32),
                        pltpu.VMEM((TILE, DK), F32),
                        pltpu.VMEM((TILE, DK), BF16),
                        pltpu.VMEM((TILE, DK), BF16),
                        pltpu.VMEM((HEADS, HEAD_K, HEAD_V), F32),
                        pltpu.VMEM((BPT, HEADS, BLOCK, BLOCK), BF16),
                        pltpu.VMEM((BPT, HEADS, HEAD_K, HEAD_V), F32),
                        pltpu.VMEM((BPT, HEADS, HEAD_K, HEAD_V), BF16)],
        compiler_params=pltpu.CompilerParams(
            dimension_semantics=("parallel", "arbitrary"),
            vmem_limit_bytes=VMEM_LIMIT),
        name="inproj_conv_gates",
    )(x, x, x, smeta, gpre, w_main, w_merge, w_lr, convw, wg, bg, woc)

    def tok2(width):
        return pl.BlockSpec((1, SUBS * TILE, width), lambda b, i: (b, i, 0))

    def const2b(shape):
        return pl.BlockSpec(shape, lambda b, i: (0,) * len(shape))

    out = pl.pallas_call(
        _gla_kernel,
        grid=(bsz, nt // SUBS),
        in_specs=[
            tok2(DK), tok2(DK), tok2(DV), tok2(DV), tok2(DV), tok2(D_MODEL),
            tok2(D_MODEL), tok2(D_MODEL),
            pl.BlockSpec((1, SUBS, 4 * BPT, DK), lambda b, i: (b, i, 0, 0)),
            const2b((HEADS, HEAD_K, HEAD_V)), const2b((1, HEAD_V)),
            const2b((DV, D_MODEL)), const2b((D_MODEL, D_MODEL)), const2b((1, D_MODEL)),
        ],
        out_specs=tok2(D_MODEL),
        out_shape=jax.ShapeDtypeStruct((bsz, seq, D_MODEL), x.dtype),
        scratch_shapes=[pltpu.VMEM((HEADS, HEAD_K, HEAD_V), F32),
                        pltpu.VMEM((TILE, DV), F32),
                        pltpu.VMEM((BPT, HEADS, BLOCK, BLOCK), BF16),
                        pltpu.VMEM((BPT, HEADS, HEAD_K, HEAD_V), F32),
                        pltpu.VMEM((BPT, HEADS, HEAD_K, HEAD_V), BF16)],
        compiler_params=pltpu.CompilerParams(
            dimension_semantics=("parallel", "arbitrary"),
            vmem_limit_bytes=VMEM_LIMIT),
        name="gla_merge_out",
    )(qf, kf, v, ob, rs, smb, a, x, d, s0, glag, wog, wo, gpost)
    return out
```
